```python
import jax, jax.numpy as jnp
from jax import lax
import numpy as np


D_MODEL = 2048
BATCH = 1
SEQ = 16384
DEPTH = 1
DEC_BATCH = 16
DEC_SEQ = 64
PAST_LEN = 2048

CHUNK = 64
N_HEADS = 32
N_KV_HEADS = 4
GROUP = N_HEADS // N_KV_HEADS
HEAD_DIM = 64
WINDOW = 128
WIN_CHUNKS = WINDOW // CHUNK
BAND = (WIN_CHUNKS + 1) * CHUNK
D_CONV = 1024
CONV_W = 3
PEER_HEADS = 8
N_KEYS = 128
N_EXPERTS = N_KEYS * N_KEYS
D_KEY = 256
D_HALF = D_KEY // 2
TOPK = 16
PEER_BLOCK = 128

EPS = 1e-6
NEG_INF = -1e30
D_Q = N_HEADS * HEAD_DIM
D_KV = N_KV_HEADS * HEAD_DIM
D_IN = D_Q + 2 * D_KV + 3 * D_CONV + 2 * D_MODEL
SPLITS = [D_Q, D_Q + D_KV, D_Q + 2 * D_KV, D_Q + 2 * D_KV + D_CONV,
          D_Q + 2 * D_KV + 2 * D_CONV, D_Q + 2 * D_KV + 3 * D_CONV,
          D_Q + 2 * D_KV + 3 * D_CONV + D_MODEL]

kernel_name = 'hybrid_swa_sink_shortconv_peer_stream_step'


def rmsnorm(x, g):
    xf = x.astype(jnp.float32)
    y = xf * lax.rsqrt(jnp.mean(xf * xf, axis=-1, keepdims=True) + EPS)
    return (y * g.astype(jnp.float32)).astype(x.dtype)


def sink_attention(q, k, v, sinks, mask):
    s = jnp.einsum('...qkgd,...skd->...kgqs', q, k).astype(jnp.float32) * (HEAD_DIM ** -0.5)
    if mask is not None:
        s = jnp.where(mask, s, NEG_INF)
    sink = sinks.astype(jnp.float32)[:, :, None, None]
    m = jnp.maximum(jnp.max(s, axis=-1, keepdims=True), sink)
    p = jnp.exp(s - m)
    probs = p / (jnp.sum(p, axis=-1, keepdims=True) + jnp.exp(sink - m))
    return jnp.einsum('...kgqs,...skd->...qkgd', probs.astype(v.dtype), v)


def banded_attention(q, k, v, sinks):
    b, t = q.shape[0], q.shape[1]
    nc = t // CHUNK
    qc = q.reshape(b, nc, CHUNK, N_KV_HEADS, GROUP, HEAD_DIM)
    pad = ((0, 0), (WINDOW, 0), (0, 0), (0, 0))
    kc = jnp.pad(k, pad).reshape(b, nc + WIN_CHUNKS, CHUNK, N_KV_HEADS, HEAD_DIM)
    vc = jnp.pad(v, pad).reshape(b, nc + WIN_CHUNKS, CHUNK, N_KV_HEADS, HEAD_DIM)
    kb = jnp.concatenate([kc[:, j:j + nc] for j in range(WIN_CHUNKS + 1)], axis=2)
    vb = jnp.concatenate([vc[:, j:j + nc] for j in range(WIN_CHUNKS + 1)], axis=2)
    valid = (jnp.arange(nc)[:, None] + jnp.arange(BAND)[None, :] // CHUNK) >= WIN_CHUNKS
    mask = valid[None, :, None, None, None, :]
    out = sink_attention(qc, kb, vb, sinks, mask)
    return out.reshape(b, t, N_KV_HEADS, GROUP, HEAD_DIM)


def peer(xn, w_query, keys1, keys2, expert_u, expert_v):
    b, t, d = xn.shape
    xt = xn.reshape(b * t, d)
    n = xt.shape[0]
    q = (xt @ w_query).reshape(n, PEER_HEADS, 2, D_HALF)
    s1 = jnp.einsum('nhd,hkd->nhk', q[:, :, 0], keys1).astype(jnp.float32)
    s2 = jnp.einsum('nhd,hkd->nhk', q[:, :, 1], keys2).astype(jnp.float32)
    v1, i1 = lax.top_k(s1, TOPK)
    v2, i2 = lax.top_k(s2, TOPK)
    cand = (v1[..., :, None] + v2[..., None, :]).reshape(n, PEER_HEADS, TOPK * TOPK)
    cidx = (i1[..., :, None] * N_KEYS + i2[..., None, :]).reshape(n, PEER_HEADS, TOPK * TOPK)
    top, pos = lax.top_k(cand, TOPK)
    idx = jnp.take_along_axis(cidx, pos, axis=-1)
    gate = jax.nn.softmax(top, axis=-1)
    npad = (-n) % PEER_BLOCK
    nb = (n + npad) // PEER_BLOCK
    xb = jnp.pad(xt, ((0, npad), (0, 0))).reshape(nb, PEER_BLOCK, d)
    ib = jnp.pad(idx, ((0, npad), (0, 0), (0, 0))).reshape(nb, PEER_BLOCK, PEER_HEADS, TOPK)
    gb = jnp.pad(gate, ((0, npad), (0, 0), (0, 0))).reshape(nb, PEER_BLOCK, PEER_HEADS, TOPK)

    def block(args):
        xk, ik, gk = args
        u = expert_u[ik]
        hdn = jnp.einsum('nd,nhkd->nhk', xk, u)
        a = jax.nn.gelu(hdn.astype(jnp.float32), approximate=False) * gk
        return jnp.einsum('nhk,nhkd->nd', a.astype(xk.dtype), expert_v[ik])

    out = lax.map(block, (xb, ib, gb)).reshape(nb * PEER_BLOCK, d)[:n]
    return out.reshape(b, t, d)


def layer(x, prev_k, prev_v, prev_conv, norm_mix_g, w_in, attn_sinks, conv_w, conv_b,
          w_proj_a, w_proj_b, w_out, norm_ffn_g, peer_w_query, peer_keys1, peer_keys2,
          peer_u, peer_v):
    b, t, _ = x.shape
    xn = rmsnorm(x, norm_mix_g)
    proj = xn @ w_in
    q, k, v, b_gate, c_gate, xv, g_a, g_b = jnp.split(proj, SPLITS, axis=-1)
    q = q.reshape(b, t, N_KV_HEADS, GROUP, HEAD_DIM)
    k = k.reshape(b, t, N_KV_HEADS, HEAD_DIM)
    v = v.reshape(b, t, N_KV_HEADS, HEAD_DIM)
    sinks = attn_sinks.reshape(N_KV_HEADS, GROUP)
    if prev_k is None:
        y_a = banded_attention(q, k, v, sinks)
        k_all, v_all = k, v
        prev_conv = jnp.zeros((b, CONV_W - 1, D_CONV), x.dtype)
    else:
        k_all = jnp.concatenate([prev_k.astype(k.dtype), k], axis=1)
        v_all = jnp.concatenate([prev_v.astype(v.dtype), v], axis=1)
        y_a = sink_attention(q, k_all, v_all, sinks, None)
    new_k = k_all[:, -WINDOW:]
    new_v = v_all[:, -WINDOW:]
    y_a = y_a.reshape(b, t, D_Q) @ w_proj_a
    u = c_gate * xv
    up = jnp.concatenate([prev_conv.astype(u.dtype), u], axis=1)
    conv = conv_b + conv_w[0] * up[:, 0:t]
    for j in range(1, CONV_W):
        conv = conv + conv_w[j] * up[:, j:j + t]
    y_b = (b_gate * conv) @ w_proj_b
    merged = jax.nn.sigmoid(g_a) * y_a + jax.nn.sigmoid(g_b) * y_b
    h = x + merged @ w_out
    h = h + peer(rmsnorm(h, norm_ffn_g), peer_w_query, peer_keys1, peer_keys2, peer_u, peer_v)
    return h, new_k, new_v, up[:, -(CONV_W - 1):]


def setup_inputs(seed: int = 0) -> dict:
    key = jax.random.key(seed)
    ks = jax.random.split(key, 24)
    f32 = jnp.float32
    nrm = lambda k, shape, scale: scale * jax.random.normal(k, shape, f32)
    return {
        'x_prompt': nrm(ks[0], (BATCH, SEQ, D_MODEL), 1.0),
        'x_sample': nrm(ks[1], (DEC_BATCH, DEC_SEQ, D_MODEL), 1.0),
        'state_attn_k': nrm(ks[2], (DEPTH, DEC_BATCH, WINDOW, N_KV_HEADS, HEAD_DIM), 1.0),
        'state_attn_v': nrm(ks[3], (DEPTH, DEC_BATCH, WINDOW, N_KV_HEADS, HEAD_DIM), 1.0),
        'state_conv': nrm(ks[4], (DEPTH, DEC_BATCH, CONV_W - 1, D_CONV), 1.0),
        'norm_mix_g': 1.0 + nrm(ks[5], (DEPTH, D_MODEL), 0.02),
        'w_in': nrm(ks[6], (DEPTH, D_MODEL, D_IN), D_MODEL ** -0.5),
        'attn_sinks': nrm(ks[7], (DEPTH, N_HEADS), 0.5),
        'conv_w': nrm(ks[8], (DEPTH, CONV_W, D_CONV), CONV_W ** -0.5),
        'conv_b': nrm(ks[9], (DEPTH, D_CONV), 0.02),
        'w_proj_a': nrm(ks[10], (DEPTH, D_Q, D_MODEL), D_Q ** -0.5),
        'w_proj_b': nrm(ks[11], (DEPTH, D_CONV, D_MODEL), D_CONV ** -0.5),
        'w_out': nrm(ks[12], (DEPTH, D_MODEL, D_MODEL), D_MODEL ** -0.5),
        'norm_ffn_g': 1.0 + nrm(ks[13], (DEPTH, D_MODEL), 0.02),
        'peer_w_query': nrm(ks[14], (DEPTH, D_MODEL, PEER_HEADS * D_KEY), D_MODEL ** -0.5),
        'peer_keys1': nrm(ks[15], (DEPTH, PEER_HEADS, N_KEYS, D_HALF), D_HALF ** -0.5),
        'peer_keys2': nrm(ks[16], (DEPTH, PEER_HEADS, N_KEYS, D_HALF), D_HALF ** -0.5),
        'peer_u': nrm(ks[17], (DEPTH, N_EXPERTS, D_MODEL), D_MODEL ** -0.5),
        'peer_v': nrm(ks[18], (DEPTH, N_EXPERTS, D_MODEL), 0.5),
        'norm_final_g': 1.0 + nrm(ks[19], (D_MODEL,), 0.02),
    }


def reference(x_prompt, x_sample, state_attn_k, state_attn_v, state_conv, norm_mix_g, w_in,
              attn_sinks, conv_w, conv_b, w_proj_a, w_proj_b, w_out, norm_ffn_g, peer_w_query,
              peer_keys1, peer_keys2, peer_u, peer_v, norm_final_g):
    yp, ys = x_prompt, x_sample
    pk, pv, pc, sk, sv, sc = [], [], [], [], [], []
    for l in range(DEPTH):
        params = (norm_mix_g[l], w_in[l], attn_sinks[l], conv_w[l], conv_b[l], w_proj_a[l],
                  w_proj_b[l], w_out[l], norm_ffn_g[l], peer_w_query[l], peer_keys1[l],
                  peer_keys2[l], peer_u[l], peer_v[l])
        yp, k1, v1, c1 = layer(yp, None, None, None, *params)
        ys, k2, v2, c2 = layer(ys, state_attn_k[l], state_attn_v[l], state_conv[l], *params)
        pk.append(k1); pv.append(v1); pc.append(c1)
        sk.append(k2); sv.append(v2); sc.append(c2)
    yp = rmsnorm(yp, norm_final_g)
    ys = rmsnorm(ys, norm_final_g)
    return (yp, ys, jnp.stack(pk), jnp.stack(pv), jnp.stack(pc), jnp.stack(sk), jnp.stack(sv), jnp.stack(sc))
```

```python
import functools
import math

import jax
import jax.numpy as jnp
from jax import lax
from jax.experimental import pallas as pl
from jax.experimental.pallas import tpu as pltpu

F32 = jnp.float32
BF16 = jnp.bfloat16

CHUNK = 64
N_HEADS = 32
N_KV_HEADS = 4
GROUP = N_HEADS // N_KV_HEADS
HEAD_DIM = 64
WINDOW = 128
BAND = WINDOW + CHUNK
CONV_W = 3
PEER_HEADS = 8
N_KEYS = 128
D_HALF = 128
TOPK = 16
EPS = 1e-6
NEG_INF = -1e30

SUBLANES = 8
LANES = 128
VMEM_LIMIT = 56 * 1024 * 1024

_NT = (((1,), (1,)), ((), ()))


def _params(*sem):
    return pltpu.CompilerParams(dimension_semantics=sem, vmem_limit_bytes=VMEM_LIMIT)


def _tile(n, pref):
    t = min(n, pref)
    while n % t:
        t //= 2
    return t


def _oddeven_merge_sort(n):
    pairs = []

    def merge(lo, m, r):
        step = r * 2
        if step < m:
            merge(lo, m, step)
            merge(lo + r, m, step)
            for i in range(lo + r, lo + m - r, step):
                pairs.append((i, i + r))
        else:
            pairs.append((lo, lo + r))

    def sort(lo, m):
        if m > 1:
            half = m // 2
            sort(lo, half)
            sort(lo + half, half)
            merge(lo, m, 1)

    sort(0, n)
    return pairs


def _bitonic_merge(n):
    pairs = []
    d = n // 2
    while d >= 1:
        for i in range(n):
            if (i & d) == 0:
                pairs.append((i, i + d))
        d //= 2
    return pairs


_SORT16 = _oddeven_merge_sort(TOPK)
_BITONIC16 = _bitonic_merge(TOPK)


def _apply_net(pairs, xs):
    xs = list(xs)
    for i, j in pairs:
        a, b = xs[i], xs[j]
        xs[i] = jnp.maximum(a, b)
        xs[j] = jnp.minimum(a, b)
    return xs


def _top16_bitonic(a, b):
    return [jnp.maximum(a[i], b[TOPK - 1 - i]) for i in range(TOPK)]


def _inproj_kernel(x_ref, g_ref, w_ref, o_ref, xn_ref):
    @pl.when(pl.program_id(1) == 0)
    def _():
        x = x_ref[...]
        r = lax.rsqrt(jnp.mean(x * x, axis=-1, keepdims=True) + EPS)
        xn_ref[...] = (x * r * g_ref[...]).astype(BF16)

    o_ref[...] = jnp.dot(xn_ref[...], w_ref[...], preferred_element_type=F32)


def _inproj(x, g, w):
    n, d = x.shape
    d_in = w.shape[1]
    tm = _tile(n, 1024)
    tn = 512
    return pl.pallas_call(
        _inproj_kernel,
        grid=(n // tm, d_in // tn),
        in_specs=[
            pl.BlockSpec((tm, d), lambda i, j: (i, 0)),
            pl.BlockSpec((1, d), lambda i, j: (0, 0)),
            pl.BlockSpec((d, tn), lambda i, j: (0, j)),
        ],
        out_specs=pl.BlockSpec((tm, tn), lambda i, j: (i, j)),
        out_shape=jax.ShapeDtypeStruct((n, d_in), F32),
        scratch_shapes=[pltpu.VMEM((tm, d), BF16)],
        compiler_params=_params("parallel", "arbitrary"),
        name="inproj",
    )(x, g.reshape(1, d), w)


def _attn_kernel(sink_ref, q_ref, k_ref, v_ref, pk_ref, pv_ref, o_ref, kall, vall, *,
                 n_chunks, tiles_per_seg):
    kall[0:WINDOW, :] = pk_ref[...].astype(BF16)
    vall[0:WINDOW, :] = pv_ref[...].astype(BF16)
    kall[WINDOW:, :] = k_ref[...].astype(BF16)
    vall[WINDOW:, :] = v_ref[...].astype(BF16)
    if tiles_per_seg:
        first_chunk = (pl.program_id(0) % tiles_per_seg) * n_chunks
        band_chunk = lax.broadcasted_iota(jnp.int32, (CHUNK, BAND), 1) // CHUNK

    def chunk_body(c, carry):
        r0 = pl.multiple_of(c * CHUNK, CHUNK)
        qc = q_ref[pl.ds(r0, CHUNK), :]
        kb = kall[pl.ds(r0, BAND), :]
        vb = vall[pl.ds(r0, BAND), :]
        if tiles_per_seg:
            valid = (first_chunk + c + band_chunk) >= (WINDOW // CHUNK)
        outs = []
        for kh in range(N_KV_HEADS):
            k_h = kb[:, kh * HEAD_DIM:(kh + 1) * HEAD_DIM]
            v_h = vb[:, kh * HEAD_DIM:(kh + 1) * HEAD_DIM]
            for g in range(GROUP):
                h = kh * GROUP + g
                q_h = qc[:, h * HEAD_DIM:(h + 1) * HEAD_DIM].astype(BF16)
                s = lax.dot_general(q_h, k_h, _NT, preferred_element_type=F32) * (HEAD_DIM ** -0.5)
                if tiles_per_seg:
                    s = jnp.where(valid, s, NEG_INF)
                sink = sink_ref[h]
                m = jnp.maximum(jnp.max(s, axis=-1, keepdims=True), sink)
                p = jnp.exp(s - m)
                probs = p / (jnp.sum(p, axis=-1, keepdims=True) + jnp.exp(sink - m))
                outs.append(jnp.dot(probs.astype(BF16), v_h, preferred_element_type=F32))
        o_ref[pl.ds(r0, CHUNK), :] = jnp.concatenate(outs, axis=1).astype(BF16)
        return carry

    lax.fori_loop(0, n_chunks, chunk_body, 0)


def _attention(sinks, q_arr, q_col, k_arr, k_col, v_arr, v_col, pk_arr, pk_map, pv_arr, pv_map,
               n, tt, tiles_per_seg):
    d_q = N_HEADS * HEAD_DIM
    d_kv = N_KV_HEADS * HEAD_DIM
    body = functools.partial(_attn_kernel, n_chunks=tt // CHUNK, tiles_per_seg=tiles_per_seg)
    return pl.pallas_call(
        body,
        grid=(n // tt,),
        in_specs=[
            pl.BlockSpec(memory_space=pltpu.SMEM),
            pl.BlockSpec((tt, d_q), lambda i: (i, q_col)),
            pl.BlockSpec((tt, d_kv), lambda i: (i, k_col)),
            pl.BlockSpec((tt, d_kv), lambda i: (i, v_col)),
            pl.BlockSpec((WINDOW, d_kv), pk_map),
            pl.BlockSpec((WINDOW, d_kv), pv_map),
        ],
        out_specs=pl.BlockSpec((tt, d_q), lambda i: (i, 0)),
        out_shape=jax.ShapeDtypeStruct((n, d_q), BF16),
        scratch_shapes=[pltpu.VMEM((WINDOW + tt, d_kv), BF16), pltpu.VMEM((WINDOW + tt, d_kv), BF16)],
        compiler_params=_params("arbitrary"),
        name="attention",
    )(sinks, q_arr, k_arr, v_arr, pk_arr, pv_arr)


def _conv_kernel(b_ref, c_ref, x_ref, hc_ref, hx_ref, w_ref, cb_ref, bc_ref, ut_ref, *, tiles_per_seg):
    u = c_ref[...] * x_ref[...]
    uh = hc_ref[...] * hx_ref[...]
    if tiles_per_seg:
        uh = jnp.where(pl.program_id(0) % tiles_per_seg == 0, 0.0, uh)
    row = lax.broadcasted_iota(jnp.int32, u.shape, 0)
    um1 = jnp.where(row == 0, uh[7:8, :], pltpu.roll(u, 1, axis=0))
    um2 = jnp.where(row == 0, uh[6:7, :], jnp.where(row == 1, uh[7:8, :], pltpu.roll(u, 2, axis=0)))
    conv = cb_ref[...] + w_ref[0:1, :] * um2
    conv = conv + w_ref[1:2, :] * um1
    conv = conv + w_ref[2:3, :] * u
    bc_ref[...] = (b_ref[...] * conv).astype(BF16)
    ut_ref[...] = u[u.shape[0] - SUBLANES:, :]


def _conv(proj, b_col, c_col, x_col, hc_arr, hc_map, hx_arr, hx_map, conv_w, conv_b, tm, tiles_per_seg):
    n = proj.shape[0]
    d_conv = conv_w.shape[1]
    tn = 512
    nh = d_conv // tn
    body = functools.partial(_conv_kernel, tiles_per_seg=tiles_per_seg)
    return pl.pallas_call(
        body,
        grid=(n // tm, nh),
        in_specs=[
            pl.BlockSpec((tm, tn), lambda i, j: (i, b_col + j)),
            pl.BlockSpec((tm, tn), lambda i, j: (i, c_col + j)),
            pl.BlockSpec((tm, tn), lambda i, j: (i, x_col + j)),
            pl.BlockSpec((SUBLANES, tn), hc_map),
            pl.BlockSpec((SUBLANES, tn), hx_map),
            pl.BlockSpec((CONV_W, tn), lambda i, j: (0, j)),
            pl.BlockSpec((1, tn), lambda i, j: (0, j)),
        ],
        out_specs=[
            pl.BlockSpec((tm, tn), lambda i, j: (i, j)),
            pl.BlockSpec((SUBLANES, tn), lambda i, j: (i, j)),
        ],
        out_shape=[
            jax.ShapeDtypeStruct((n, d_conv), BF16),
            jax.ShapeDtypeStruct((n // tm * SUBLANES, d_conv), F32),
        ],
        compiler_params=_params("arbitrary", "arbitrary"),
        name="conv",
    )(proj, proj, proj, hc_arr, hx_arr, conv_w, conv_b.reshape(1, d_conv))


def _merge_kernel(a_ref, bc_ref, wa_ref, wb_ref, ga_ref, gb_ref, o_ref):
    ya = jnp.dot(a_ref[...], wa_ref[...], preferred_element_type=F32)
    yb = jnp.dot(bc_ref[...], wb_ref[...], preferred_element_type=F32)
    o_ref[...] = (jax.nn.sigmoid(ga_ref[...]) * ya + jax.nn.sigmoid(gb_ref[...]) * yb).astype(BF16)


def _merge(attn, bc, wa, wb, proj, ga_col, gb_col):
    n, d_q = attn.shape
    d_conv = bc.shape[1]
    d = wa.shape[1]
    tm = _tile(n, 1024)
    tn = 512
    return pl.pallas_call(
        _merge_kernel,
        grid=(n // tm, d // tn),
        in_specs=[
            pl.BlockSpec((tm, d_q), lambda i, j: (i, 0)),
            pl.BlockSpec((tm, d_conv), lambda i, j: (i, 0)),
            pl.BlockSpec((d_q, tn), lambda i, j: (0, j)),
            pl.BlockSpec((d_conv, tn), lambda i, j: (0, j)),
            pl.BlockSpec((tm, tn), lambda i, j: (i, ga_col + j)),
            pl.BlockSpec((tm, tn), lambda i, j: (i, gb_col + j)),
        ],
        out_specs=pl.BlockSpec((tm, tn), lambda i, j: (i, j)),
        out_shape=jax.ShapeDtypeStruct((n, d), BF16),
        compiler_params=_params("parallel", "arbitrary"),
        name="merge",
    )(attn, bc, wa, wb, proj, proj)


def _outproj_kernel(m_ref, x_ref, w_ref, g_ref, h_ref, xn_ref):
    h = x_ref[...] + jnp.dot(m_ref[...], w_ref[...], preferred_element_type=F32)
    h_ref[...] = h
    r = lax.rsqrt(jnp.mean(h * h, axis=-1, keepdims=True) + EPS)
    xn_ref[...] = (h * r * g_ref[...]).astype(BF16)


def _outproj(merged, x, w, g):
    n, d = x.shape
    tm = _tile(n, 512)
    return pl.pallas_call(
        _outproj_kernel,
        grid=(n // tm,),
        in_specs=[
            pl.BlockSpec((tm, d), lambda i: (i, 0)),
            pl.BlockSpec((tm, d), lambda i: (i, 0)),
            pl.BlockSpec((d, d), lambda i: (0, 0)),
            pl.BlockSpec((1, d), lambda i: (0, 0)),
        ],
        out_specs=[pl.BlockSpec((tm, d), lambda i: (i, 0)), pl.BlockSpec((tm, d), lambda i: (i, 0))],
        out_shape=[jax.ShapeDtypeStruct((n, d), F32), jax.ShapeDtypeStruct((n, d), BF16)],
        compiler_params=_params("parallel"),
        name="outproj",
    )(merged, x, w, g.reshape(1, d))


def _pair_threshold(v1, v2):
    c = lambda i, j: v1[i] + v2[j]
    row0 = [c(0, j) for j in range(TOPK)]
    grp1 = [c(1, j) for j in range(8)] + [c(2, j) for j in range(5)] + [c(3, j) for j in range(3)]
    grp2 = ([c(3, 3)] + [c(4, j) for j in range(3)] + [c(5, 0), c(5, 1), c(6, 0), c(6, 1), c(7, 0), c(7, 1)]
            + [c(i, 0) for i in range(8, 14)])
    top = _apply_net(_BITONIC16, _top16_bitonic(row0, _apply_net(_SORT16, grp1)))
    top = _apply_net(_BITONIC16, _top16_bitonic(top, _apply_net(_SORT16, grp2)))
    top[TOPK - 1] = jnp.maximum(top[TOPK - 1], c(14, 0))
    top[TOPK - 2] = jnp.maximum(top[TOPK - 2], c(15, 0))
    tau = functools.reduce(jnp.minimum, top)
    m = row0[0]
    z = functools.reduce(lambda a, b: a + b, [jnp.exp(t - m) for t in top])
    return tau, z


def _route_kernel(xn_ref, wq_ref, k1_ref, k2_ref, xt_ref, s2_ref, e2_ref, s1_ref, e1_ref, tau_ref,
                  q_scr, s_scr, top_scr):
    tb = xn_ref.shape[0]
    q_scr[...] = jnp.dot(xn_ref[...], wq_ref[...], preferred_element_type=F32).astype(BF16)
    xt_ref[...] = xn_ref[...].astype(F32).T.astype(BF16)
    for h in range(PEER_HEADS):
        rows = slice(h * N_KEYS, (h + 1) * N_KEYS)
        q1 = q_scr[:, 2 * h * D_HALF:(2 * h + 1) * D_HALF]
        q2 = q_scr[:, (2 * h + 1) * D_HALF:(2 * h + 2) * D_HALF]
        s_scr[0, rows, :] = lax.dot_general(k1_ref[h], q1, _NT, preferred_element_type=F32)
        s_scr[1, rows, :] = lax.dot_general(k2_ref[h], q2, _NT, preferred_element_type=F32)

    def lane_block(lb, carry):
        cols = pl.ds(pl.multiple_of(lb * LANES, LANES), LANES)
        for side in range(2):
            for h in range(PEER_HEADS):
                lst = [s_scr[side, pl.ds(h * N_KEYS + SUBLANES * r, SUBLANES), cols]
                       for r in range(N_KEYS // SUBLANES)]
                lst = _apply_net(_SORT16, lst)
                for d in (4, 2, 1):
                    other = [pltpu.roll(x, d, axis=0) for x in lst]
                    lst = _apply_net(_BITONIC16, _top16_bitonic(lst, other))
                for i in range(TOPK):
                    top_scr[side, pl.ds(i * PEER_HEADS + h, 1), cols] = lst[i][0:1, :]
        v1 = [top_scr[0, pl.ds(i * PEER_HEADS, PEER_HEADS), cols] for i in range(TOPK)]
        v2 = [top_scr[1, pl.ds(i * PEER_HEADS, PEER_HEADS), cols] for i in range(TOPK)]
        tau, z = _pair_threshold(v1, v2)
        tau_ref[:, cols] = tau
        for h in range(PEER_HEADS):
            rows = slice(h * N_KEYS, (h + 1) * N_KEYS)
            s1 = s_scr[0, rows, cols]
            s2 = s_scr[1, rows, cols]
            s1_ref[rows, cols] = s1
            e1_ref[rows, cols] = jnp.exp(s1 - v1[0][h:h + 1, :])
            s2_ref[rows, cols] = s2
            e2_ref[rows, cols] = jnp.exp(s2 - v2[0][h:h + 1, :]) / z[h:h + 1, :]
        return carry

    lax.fori_loop(0, tb // LANES, lane_block, 0)


def _route(xn, wq, k1, k2):
    n, d = xn.shape
    dq = wq.shape[1]
    tb = _tile(n, 512)
    rows = PEER_HEADS * N_KEYS
    tok = lambda i: (0, i)
    return pl.pallas_call(
        _route_kernel,
        grid=(n // tb,),
        in_specs=[
            pl.BlockSpec((tb, d), lambda i: (i, 0)),
            pl.BlockSpec((d, dq), lambda i: (0, 0)),
            pl.BlockSpec((PEER_HEADS, N_KEYS, D_HALF), lambda i: (0, 0, 0)),
            pl.BlockSpec((PEER_HEADS, N_KEYS, D_HALF), lambda i: (0, 0, 0)),
        ],
        out_specs=([pl.BlockSpec((d, tb), tok)] + [pl.BlockSpec((rows, tb), tok)] * 4
                   + [pl.BlockSpec((PEER_HEADS, tb), tok)]),
        out_shape=([jax.ShapeDtypeStruct((d, n), BF16)] + [jax.ShapeDtypeStruct((rows, n), F32)] * 4
                   + [jax.ShapeDtypeStruct((PEER_HEADS, n), F32)]),
        scratch_shapes=[
            pltpu.VMEM((tb, dq), BF16),
            pltpu.VMEM((2, rows, tb), F32),
            pltpu.VMEM((2, TOPK * PEER_HEADS, tb), F32),
        ],
        compiler_params=_params("parallel"),
        name="peer_route",
    )(xn, wq, k1, k2)


def _peer_kernel(xt_ref, s2_ref, e2_ref, s1_ref, e1_ref, tau_ref, u_ref, vt_ref, o_ref, h_scr, a_scr):
    eb, tb = h_scr.shape
    n_i1 = eb // N_KEYS
    rb = 2 * SUBLANES

    @pl.when(pl.program_id(1) == 0)
    def _():
        o_ref[...] = jnp.zeros_like(o_ref)

    h_scr[...] = jnp.dot(u_ref[...], xt_ref[...], preferred_element_type=F32)

    def lane_block(lb, carry):
        cols = pl.ds(pl.multiple_of(lb * LANES, LANES), LANES)
        for r0 in range(0, N_KEYS, rb):
            gate = [None] * n_i1
            for h in range(PEER_HEADS):
                s2 = s2_ref[h, r0:r0 + rb, cols]
                e2 = e2_ref[h, r0:r0 + rb, cols]
                tau = tau_ref[h:h + 1, cols]
                for a in range(n_i1):
                    sel = (s1_ref[h, a:a + 1, cols] + s2) >= tau
                    t = jnp.where(sel, e2, 0.0) * e1_ref[h, a:a + 1, cols]
                    gate[a] = t if gate[a] is None else gate[a] + t
            for a in range(n_i1):
                hid = h_scr[a * N_KEYS + r0:a * N_KEYS + r0 + rb, cols]
                act = 0.5 * hid * (1.0 + lax.erf(hid * (1.0 / math.sqrt(2.0))))
                a_scr[a * N_KEYS + r0:a * N_KEYS + r0 + rb, cols] = (act * gate[a]).astype(BF16)
        return carry

    lax.fori_loop(0, tb // LANES, lane_block, 0)
    o_ref[...] += jnp.dot(vt_ref[...], a_scr[...], preferred_element_type=F32)


def _peer(xt, s2, e2, s1, e1, tau, u, vt):
    d, n = xt.shape
    n_exp = u.shape[0]
    tb = _tile(n, 512)
    eb = 1024
    n_i1 = eb // N_KEYS
    r3 = lambda a: a.reshape(PEER_HEADS, N_KEYS, n)
    return pl.pallas_call(
        _peer_kernel,
        grid=(n // tb, n_exp // eb),
        in_specs=[
            pl.BlockSpec((d, tb), lambda i, j: (0, i)),
            pl.BlockSpec((PEER_HEADS, N_KEYS, tb), lambda i, j: (0, 0, i)),
            pl.BlockSpec((PEER_HEADS, N_KEYS, tb), lambda i, j: (0, 0, i)),
            pl.BlockSpec((PEER_HEADS, n_i1, tb), lambda i, j: (0, j, i)),
            pl.BlockSpec((PEER_HEADS, n_i1, tb), lambda i, j: (0, j, i)),
            pl.BlockSpec((PEER_HEADS, tb), lambda i, j: (0, i)),
            pl.BlockSpec((eb, d), lambda i, j: (j, 0)),
            pl.BlockSpec((d, eb), lambda i, j: (0, j)),
        ],
        out_specs=pl.BlockSpec((d, tb), lambda i, j: (0, i)),
        out_shape=jax.ShapeDtypeStruct((d, n), F32),
        scratch_shapes=[pltpu.VMEM((eb, tb), F32), pltpu.VMEM((eb, tb), BF16)],
        compiler_params=_params("parallel", "arbitrary"),
        name="peer_dense",
    )(xt, r3(s2), r3(e2), r3(s1), r3(e1), tau, u, vt)


def _final_kernel(h_ref, pt_ref, g_ref, y_ref, *, normalize):
    y = h_ref[...] + pt_ref[...].T
    if normalize:
        r = lax.rsqrt(jnp.mean(y * y, axis=-1, keepdims=True) + EPS)
        y = y * r * g_ref[...]
    y_ref[...] = y


def _final(h, pt, g, normalize):
    n, d = h.shape
    tm = _tile(n, 512)
    return pl.pallas_call(
        functools.partial(_final_kernel, normalize=normalize),
        grid=(n // tm,),
        in_specs=[
            pl.BlockSpec((tm, d), lambda i: (i, 0)),
            pl.BlockSpec((d, tm), lambda i: (0, i)),
            pl.BlockSpec((1, d), lambda i: (0, 0)),
        ],
        out_specs=pl.BlockSpec((tm, d), lambda i: (i, 0)),
        out_shape=jax.ShapeDtypeStruct((n, d), F32),
        compiler_params=_params("parallel"),
        name="final",
    )(h, pt, g.reshape(1, d))


def _layer(x, seg_len, prev_k, prev_v, prev_conv, w, norm_final_g, last):
    n, d = x.shape
    n_seg = n // seg_len
    d_q = N_HEADS * HEAD_DIM
    d_kv = N_KV_HEADS * HEAD_DIM
    d_conv = w["conv_w"].shape[1]
    proj = _inproj(x, w["norm_mix_g"], w["w_in"])
    k_col, v_col = d_q // d_kv, d_q // d_kv + 1
    o_b = d_q + 2 * d_kv
    b_col, c_col, x_col = o_b // 512, (o_b + d_conv) // 512, (o_b + 2 * d_conv) // 512
    ga_col, gb_col = (o_b + 3 * d_conv) // 512, (o_b + 3 * d_conv + d) // 512
    assert o_b % 512 == 0 and d_conv % 512 == 0 and d % 512 == 0 and d_q % d_kv == 0

    if prev_k is None:
        tt = _tile(seg_len, 512)
        assert tt % WINDOW == 0
        tps = seg_len // tt
        halo = lambda col: (lambda i: (jnp.maximum(i * (tt // WINDOW) - 1, 0), col))
        attn = _attention(w["attn_sinks"], proj, 0, proj, k_col, proj, v_col,
                          proj, halo(k_col), proj, halo(v_col), n, tt, tps)
        tc = tt
        chalo = lambda col: (lambda i, j: (jnp.maximum(i * (tc // SUBLANES) - 1, 0), col + j))
        bc, utail = _conv(proj, b_col, c_col, x_col, proj, chalo(c_col), proj, chalo(x_col),
                          w["conv_w"], w["conv_b"], tc, seg_len // tc)
        new_k = proj.reshape(n_seg, seg_len, -1)[:, -WINDOW:, d_q:d_q + d_kv]
        new_v = proj.reshape(n_seg, seg_len, -1)[:, -WINDOW:, d_q + d_kv:d_q + 2 * d_kv]
    else:
        assert seg_len == CHUNK
        pk = prev_k.reshape(n_seg * WINDOW, d_kv)
        pv = prev_v.reshape(n_seg * WINDOW, d_kv)
        seg = lambda i: (i, 0)
        attn = _attention(w["attn_sinks"], proj, 0, proj, k_col, proj, v_col,
                          pk, seg, pv, seg, n, seg_len, 0)
        hist = jnp.pad(prev_conv, ((0, 0), (SUBLANES - (CONV_W - 1), 0), (0, 0))).reshape(n_seg * SUBLANES, d_conv)
        hmap = lambda i, j: (i, j)
        bc, utail = _conv(proj, b_col, c_col, x_col, hist, hmap, jnp.ones_like(hist), hmap,
                          w["conv_w"], w["conv_b"], seg_len, 0)
        k_new = proj[:, d_q:d_q + d_kv].reshape(n_seg, seg_len, d_kv)
        v_new = proj[:, d_q + d_kv:d_q + 2 * d_kv].reshape(n_seg, seg_len, d_kv)
        new_k = jnp.concatenate([prev_k.reshape(n_seg, WINDOW, d_kv), k_new], axis=1)[:, -WINDOW:]
        new_v = jnp.concatenate([prev_v.reshape(n_seg, WINDOW, d_kv), v_new], axis=1)[:, -WINDOW:]
    new_conv = utail.reshape(n_seg, -1, SUBLANES, d_conv)[:, -1, SUBLANES - (CONV_W - 1):]

    merged = _merge(attn, bc, w["w_proj_a"], w["w_proj_b"], proj, ga_col, gb_col)
    h, xn = _outproj(merged, x, w["w_out"], w["norm_ffn_g"])
    xt, s2, e2, s1, e1, tau = _route(xn, w["peer_w_query"], w["peer_keys1"], w["peer_keys2"])
    pt = _peer(xt, s2, e2, s1, e1, tau, w["peer_u"], w["peer_vt"])
    y = _final(h, pt, norm_final_g, last)
    shape5 = (n_seg, WINDOW, N_KV_HEADS, HEAD_DIM)
    return y, new_k.reshape(shape5), new_v.reshape(shape5), new_conv


def kernel(x_prompt, x_sample, state_attn_k, state_attn_v, state_conv, norm_mix_g, w_in, attn_sinks, conv_w,
           conv_b, w_proj_a, w_proj_b, w_out, norm_ffn_g, peer_w_query, peer_keys1, peer_keys2, peer_u, peer_v,
           norm_final_g):
    depth = w_in.shape[0]
    bp, sp, d = x_prompt.shape
    bs, ss, _ = x_sample.shape
    yp = x_prompt.reshape(bp * sp, d)
    ys = x_sample.reshape(bs * ss, d)
    outs = [[] for _ in range(6)]
    for l in range(depth):
        w = dict(
            norm_mix_g=norm_mix_g[l], w_in=w_in[l].astype(BF16), attn_sinks=attn_sinks[l],
            conv_w=conv_w[l], conv_b=conv_b[l], w_proj_a=w_proj_a[l].astype(BF16),
            w_proj_b=w_proj_b[l].astype(BF16), w_out=w_out[l].astype(BF16), norm_ffn_g=norm_ffn_g[l],
            peer_w_query=peer_w_query[l].astype(BF16), peer_keys1=peer_keys1[l].astype(BF16),
            peer_keys2=peer_keys2[l].astype(BF16), peer_u=peer_u[l].astype(BF16),
            peer_vt=peer_v[l].astype(BF16).T,
        )
        last = l == depth - 1
        yp, k1, v1, c1 = _layer(yp, sp, None, None, None, w, norm_final_g, last)
        ys, k2, v2, c2 = _layer(ys, ss, state_attn_k[l], state_attn_v[l], state_conv[l], w, norm_final_g, last)
        for lst, val in zip(outs, (k1, v1, c1, k2, v2, c2)):
            lst.append(val)
    return (yp.reshape(bp, sp, d), ys.reshape(bs, ss, d)) + tuple(jnp.stack(o) for o in outs)
```

```python
import functools
import math

import jax
import jax.numpy as jnp
from jax import lax
from jax.experimental import pallas as pl
from jax.experimental.pallas import tpu as pltpu

F32 = jnp.float32
BF16 = jnp.bfloat16

CHUNK = 64
N_HEADS = 32
N_KV_HEADS = 4
GROUP = N_HEADS // N_KV_HEADS
HEAD_DIM = 64
WINDOW = 128
BAND = WINDOW + CHUNK
CONV_W = 3
PEER_HEADS = 8
N_KEYS = 128
D_HALF = 128
TOPK = 16
EPS = 1e-6
NEG_INF = -1e30

SUBLANES = 8
LANES = 128
VMEM_LIMIT = 56 * 1024 * 1024

_NT = (((1,), (1,)), ((), ()))


def _params(*sem, flags=None):
    return pltpu.CompilerParams(dimension_semantics=sem, vmem_limit_bytes=VMEM_LIMIT, flags=flags)


def _tile(n, pref):
    t = min(n, pref)
    while n % t:
        t //= 2
    return t


def _oddeven_merge_sort(n):
    pairs = []

    def merge(lo, m, r):
        step = r * 2
        if step < m:
            merge(lo, m, step)
            merge(lo + r, m, step)
            for i in range(lo + r, lo + m - r, step):
                pairs.append((i, i + r))
        else:
            pairs.append((lo, lo + r))

    def sort(lo, m):
        if m > 1:
            half = m // 2
            sort(lo, half)
            sort(lo + half, half)
            merge(lo, m, 1)

    sort(0, n)
    return pairs


def _bitonic_merge(n):
    pairs = []
    d = n // 2
    while d >= 1:
        for i in range(n):
            if (i & d) == 0:
                pairs.append((i, i + d))
        d //= 2
    return pairs


_SORT16 = _oddeven_merge_sort(TOPK)
_BITONIC16 = _bitonic_merge(TOPK)


def _apply_net(pairs, xs):
    xs = list(xs)
    for i, j in pairs:
        a, b = xs[i], xs[j]
        xs[i] = jnp.maximum(a, b)
        xs[j] = jnp.minimum(a, b)
    return xs


def _top16_bitonic(a, b):
    return [jnp.maximum(a[i], b[TOPK - 1 - i]) for i in range(TOPK)]


def _inproj_kernel(x_ref, g_ref, w_ref, o_ref, xn_ref):
    @pl.when(pl.program_id(1) == 0)
    def _():
        x = x_ref[...]
        r = lax.rsqrt(jnp.mean(x * x, axis=-1, keepdims=True) + EPS)
        xn_ref[...] = (x * r * g_ref[...]).astype(BF16)

    o_ref[...] = jnp.dot(xn_ref[...], w_ref[...], preferred_element_type=F32)


def _inproj(x, g, w):
    n, d = x.shape
    d_in = w.shape[1]
    tm = _tile(n, 1024)
    tn = 512
    return pl.pallas_call(
        _inproj_kernel,
        grid=(n // tm, d_in // tn),
        in_specs=[
            pl.BlockSpec((tm, d), lambda i, j: (i, 0)),
            pl.BlockSpec((1, d), lambda i, j: (0, 0)),
            pl.BlockSpec((d, tn), lambda i, j: (0, j)),
        ],
        out_specs=pl.BlockSpec((tm, tn), lambda i, j: (i, j)),
        out_shape=jax.ShapeDtypeStruct((n, d_in), F32),
        scratch_shapes=[pltpu.VMEM((tm, d), BF16)],
        compiler_params=_params("parallel", "arbitrary"),
        name="inproj",
    )(x, g.reshape(1, d), w)


def _attn_kernel(sink_ref, q_ref, k_ref, v_ref, pk_ref, pv_ref, o_ref, kall, vall, *,
                 n_chunks, tiles_per_seg):
    kall[0:WINDOW, :] = pk_ref[...].astype(BF16)
    vall[0:WINDOW, :] = pv_ref[...].astype(BF16)
    kall[WINDOW:, :] = k_ref[...].astype(BF16)
    vall[WINDOW:, :] = v_ref[...].astype(BF16)
    nq = GROUP * CHUNK
    if tiles_per_seg:
        first_chunk = (pl.program_id(0) % tiles_per_seg) * n_chunks
        band_chunk = lax.broadcasted_iota(jnp.int32, (BAND, nq), 0) // CHUNK
    q_group = lax.broadcasted_iota(jnp.int32, (1, nq), 1) // CHUNK
    sink_rows = []
    for kh in range(N_KV_HEADS):
        row = jnp.zeros((1, nq), F32)
        for g in range(GROUP):
            row = jnp.where(q_group == g, sink_ref[kh * GROUP + g], row)
        sink_rows.append(row)

    def chunk_body(c, carry):
        r0 = pl.multiple_of(c * CHUNK, CHUNK)
        qc = q_ref[pl.ds(r0, CHUNK), :]
        kb = kall[pl.ds(r0, BAND), :]
        vb = vall[pl.ds(r0, BAND), :]
        if tiles_per_seg:
            valid = (first_chunk + c + band_chunk) >= (WINDOW // CHUNK)
        outs = []
        for kh in range(N_KV_HEADS):
            k_h = kb[:, kh * HEAD_DIM:(kh + 1) * HEAD_DIM]
            v_h = vb[:, kh * HEAD_DIM:(kh + 1) * HEAD_DIM]
            q_h = jnp.concatenate(
                [qc[:, (kh * GROUP + g) * HEAD_DIM:(kh * GROUP + g + 1) * HEAD_DIM] for g in range(GROUP)],
                axis=0).astype(BF16)
            s = lax.dot_general(k_h, q_h, _NT, preferred_element_type=F32) * (HEAD_DIM ** -0.5)
            if tiles_per_seg:
                s = jnp.where(valid, s, NEG_INF)
            sink = sink_rows[kh]
            m = jnp.maximum(jnp.max(s, axis=0, keepdims=True), sink)
            p = jnp.exp(s - m)
            probs = p / (jnp.sum(p, axis=0, keepdims=True) + jnp.exp(sink - m))
            o_t = lax.dot_general(v_h, probs.astype(BF16), (((0,), (0,)), ((), ())),
                                  preferred_element_type=F32)
            o = o_t.T
            outs += [o[g * CHUNK:(g + 1) * CHUNK, :] for g in range(GROUP)]
        o_ref[pl.ds(r0, CHUNK), :] = jnp.concatenate(outs, axis=1).astype(BF16)
        return carry

    lax.fori_loop(0, n_chunks, chunk_body, 0)


def _attention(sinks, q_arr, q_col, k_arr, k_col, v_arr, v_col, pk_arr, pk_map, pv_arr, pv_map,
               n, tt, tiles_per_seg):
    d_q = N_HEADS * HEAD_DIM
    d_kv = N_KV_HEADS * HEAD_DIM
    body = functools.partial(_attn_kernel, n_chunks=tt // CHUNK, tiles_per_seg=tiles_per_seg)
    return pl.pallas_call(
        body,
        grid=(n // tt,),
        in_specs=[
            pl.BlockSpec(memory_space=pltpu.SMEM),
            pl.BlockSpec((tt, d_q), lambda i: (i, q_col)),
            pl.BlockSpec((tt, d_kv), lambda i: (i, k_col)),
            pl.BlockSpec((tt, d_kv), lambda i: (i, v_col)),
            pl.BlockSpec((WINDOW, d_kv), pk_map),
            pl.BlockSpec((WINDOW, d_kv), pv_map),
        ],
        out_specs=pl.BlockSpec((tt, d_q), lambda i: (i, 0)),
        out_shape=jax.ShapeDtypeStruct((n, d_q), BF16),
        scratch_shapes=[pltpu.VMEM((WINDOW + tt, d_kv), BF16), pltpu.VMEM((WINDOW + tt, d_kv), BF16)],
        compiler_params=_params("arbitrary"),
        name="attention",
    )(sinks, q_arr, k_arr, v_arr, pk_arr, pv_arr)


def _conv_kernel(b_ref, c_ref, x_ref, hc_ref, hx_ref, w_ref, cb_ref, bc_ref, ut_ref, *, tiles_per_seg):
    u = c_ref[...] * x_ref[...]
    uh = hc_ref[...] * hx_ref[...]
    if tiles_per_seg:
        uh = jnp.where(pl.program_id(0) % tiles_per_seg == 0, 0.0, uh)
    row = lax.broadcasted_iota(jnp.int32, u.shape, 0)
    um1 = jnp.where(row == 0, uh[7:8, :], pltpu.roll(u, 1, axis=0))
    um2 = jnp.where(row == 0, uh[6:7, :], jnp.where(row == 1, uh[7:8, :], pltpu.roll(u, 2, axis=0)))
    conv = cb_ref[...] + w_ref[0:1, :] * um2
    conv = conv + w_ref[1:2, :] * um1
    conv = conv + w_ref[2:3, :] * u
    bc_ref[...] = (b_ref[...] * conv).astype(BF16)
    ut_ref[...] = u[u.shape[0] - SUBLANES:, :]


def _conv(proj, b_col, c_col, x_col, hc_arr, hc_map, hx_arr, hx_map, conv_w, conv_b, tm, tiles_per_seg):
    n = proj.shape[0]
    d_conv = conv_w.shape[1]
    tn = 512
    nh = d_conv // tn
    body = functools.partial(_conv_kernel, tiles_per_seg=tiles_per_seg)
    return pl.pallas_call(
        body,
        grid=(n // tm, nh),
        in_specs=[
            pl.BlockSpec((tm, tn), lambda i, j: (i, b_col + j)),
            pl.BlockSpec((tm, tn), lambda i, j: (i, c_col + j)),
            pl.BlockSpec((tm, tn), lambda i, j: (i, x_col + j)),
            pl.BlockSpec((SUBLANES, tn), hc_map),
            pl.BlockSpec((SUBLANES, tn), hx_map),
            pl.BlockSpec((CONV_W, tn), lambda i, j: (0, j)),
            pl.BlockSpec((1, tn), lambda i, j: (0, j)),
        ],
        out_specs=[
            pl.BlockSpec((tm, tn), lambda i, j: (i, j)),
            pl.BlockSpec((SUBLANES, tn), lambda i, j: (i, j)),
        ],
        out_shape=[
            jax.ShapeDtypeStruct((n, d_conv), BF16),
            jax.ShapeDtypeStruct((n // tm * SUBLANES, d_conv), F32),
        ],
        compiler_params=_params("arbitrary", "arbitrary"),
        name="conv",
    )(proj, proj, proj, hc_arr, hx_arr, conv_w, conv_b.reshape(1, d_conv))


def _merge_kernel(a_ref, bc_ref, wa_ref, wb_ref, ga_ref, gb_ref, o_ref):
    ya = jnp.dot(a_ref[...], wa_ref[...], preferred_element_type=F32)
    yb = jnp.dot(bc_ref[...], wb_ref[...], preferred_element_type=F32)
    o_ref[...] = (jax.nn.sigmoid(ga_ref[...]) * ya + jax.nn.sigmoid(gb_ref[...]) * yb).astype(BF16)


def _merge(attn, bc, wa, wb, proj, ga_col, gb_col):
    n, d_q = attn.shape
    d_conv = bc.shape[1]
    d = wa.shape[1]
    tm = _tile(n, 1024)
    tn = 512
    return pl.pallas_call(
        _merge_kernel,
        grid=(n // tm, d // tn),
        in_specs=[
            pl.BlockSpec((tm, d_q), lambda i, j: (i, 0)),
            pl.BlockSpec((tm, d_conv), lambda i, j: (i, 0)),
            pl.BlockSpec((d_q, tn), lambda i, j: (0, j)),
            pl.BlockSpec((d_conv, tn), lambda i, j: (0, j)),
            pl.BlockSpec((tm, tn), lambda i, j: (i, ga_col + j)),
            pl.BlockSpec((tm, tn), lambda i, j: (i, gb_col + j)),
        ],
        out_specs=pl.BlockSpec((tm, tn), lambda i, j: (i, j)),
        out_shape=jax.ShapeDtypeStruct((n, d), BF16),
        compiler_params=_params("parallel", "arbitrary"),
        name="merge",
    )(attn, bc, wa, wb, proj, proj)


def _outproj_kernel(m_ref, x_ref, w_ref, g_ref, h_ref, xn_ref):
    h = x_ref[...] + jnp.dot(m_ref[...], w_ref[...], preferred_element_type=F32)
    h_ref[...] = h
    r = lax.rsqrt(jnp.mean(h * h, axis=-1, keepdims=True) + EPS)
    xn_ref[...] = (h * r * g_ref[...]).astype(BF16)


def _outproj(merged, x, w, g):
    n, d = x.shape
    tm = _tile(n, 512)
    return pl.pallas_call(
        _outproj_kernel,
        grid=(n // tm,),
        in_specs=[
            pl.BlockSpec((tm, d), lambda i: (i, 0)),
            pl.BlockSpec((tm, d), lambda i: (i, 0)),
            pl.BlockSpec((d, d), lambda i: (0, 0)),
            pl.BlockSpec((1, d), lambda i: (0, 0)),
        ],
        out_specs=[pl.BlockSpec((tm, d), lambda i: (i, 0)), pl.BlockSpec((tm, d), lambda i: (i, 0))],
        out_shape=[jax.ShapeDtypeStruct((n, d), F32), jax.ShapeDtypeStruct((n, d), BF16)],
        compiler_params=_params("parallel"),
        name="outproj",
    )(merged, x, w, g.reshape(1, d))


def _pair_threshold(v1, v2):
    c = lambda i, j: v1[i] + v2[j]
    row0 = [c(0, j) for j in range(TOPK)]
    grp1 = [c(1, j) for j in range(8)] + [c(2, j) for j in range(5)] + [c(3, j) for j in range(3)]
    grp2 = ([c(3, 3)] + [c(4, j) for j in range(3)] + [c(5, 0), c(5, 1), c(6, 0), c(6, 1), c(7, 0), c(7, 1)]
            + [c(i, 0) for i in range(8, 14)])
    top = _apply_net(_BITONIC16, _top16_bitonic(row0, _apply_net(_SORT16, grp1)))
    top = _apply_net(_BITONIC16, _top16_bitonic(top, _apply_net(_SORT16, grp2)))
    top[TOPK - 1] = jnp.maximum(top[TOPK - 1], c(14, 0))
    top[TOPK - 2] = jnp.maximum(top[TOPK - 2], c(15, 0))
    tau = functools.reduce(jnp.minimum, top)
    m = row0[0]
    z = functools.reduce(lambda a, b: a + b, [jnp.exp(t - m) for t in top])
    return tau, z


def _route_kernel(xn_ref, wq_ref, k1_ref, k2_ref, xt_ref, s2_ref, e2_ref, s1_ref, e1_ref, tau_ref,
                  q_scr, s_scr, top_scr):
    tb = xn_ref.shape[0]
    q_scr[...] = jnp.dot(xn_ref[...], wq_ref[...], preferred_element_type=F32).astype(BF16)
    xt_ref[...] = xn_ref[...].astype(F32).T.astype(BF16)
    for h in range(PEER_HEADS):
        rows = slice(h * N_KEYS, (h + 1) * N_KEYS)
        q1 = q_scr[:, 2 * h * D_HALF:(2 * h + 1) * D_HALF]
        q2 = q_scr[:, (2 * h + 1) * D_HALF:(2 * h + 2) * D_HALF]
        s_scr[0, rows, :] = lax.dot_general(k1_ref[h], q1, _NT, preferred_element_type=F32)
        s_scr[1, rows, :] = lax.dot_general(k2_ref[h], q2, _NT, preferred_element_type=F32)

    def lane_block(lb, carry):
        cols = pl.ds(pl.multiple_of(lb * LANES, LANES), LANES)
        for side in range(2):
            for h in range(PEER_HEADS):
                lst = [s_scr[side, pl.ds(h * N_KEYS + SUBLANES * r, SUBLANES), cols]
                       for r in range(N_KEYS // SUBLANES)]
                lst = _apply_net(_SORT16, lst)
                for d in (4, 2, 1):
                    other = [pltpu.roll(x, d, axis=0) for x in lst]
                    lst = _apply_net(_BITONIC16, _top16_bitonic(lst, other))
                for i in range(TOPK):
                    top_scr[side, pl.ds(i * PEER_HEADS + h, 1), cols] = lst[i][0:1, :]
        v1 = [top_scr[0, pl.ds(i * PEER_HEADS, PEER_HEADS), cols] for i in range(TOPK)]
        v2 = [top_scr[1, pl.ds(i * PEER_HEADS, PEER_HEADS), cols] for i in range(TOPK)]
        tau, z = _pair_threshold(v1, v2)
        tau_ref[:, cols] = tau
        for h in range(PEER_HEADS):
            rows = slice(h * N_KEYS, (h + 1) * N_KEYS)
            s1 = s_scr[0, rows, cols]
            s2 = s_scr[1, rows, cols]
            s1_ref[rows, cols] = s1
            e1_ref[rows, cols] = jnp.exp(s1 - v1[0][h:h + 1, :])
            s2_ref[rows, cols] = s2
            e2_ref[rows, cols] = jnp.exp(s2 - v2[0][h:h + 1, :]) / z[h:h + 1, :]
        return carry

    lax.fori_loop(0, tb // LANES, lane_block, 0)


def _route(xn, wq, k1, k2):
    n, d = xn.shape
    dq = wq.shape[1]
    tb = _tile(n, 512)
    rows = PEER_HEADS * N_KEYS
    tok = lambda i: (0, i)
    return pl.pallas_call(
        _route_kernel,
        grid=(n // tb,),
        in_specs=[
            pl.BlockSpec((tb, d), lambda i: (i, 0)),
            pl.BlockSpec((d, dq), lambda i: (0, 0)),
            pl.BlockSpec((PEER_HEADS, N_KEYS, D_HALF), lambda i: (0, 0, 0)),
            pl.BlockSpec((PEER_HEADS, N_KEYS, D_HALF), lambda i: (0, 0, 0)),
        ],
        out_specs=([pl.BlockSpec((d, tb), tok)] + [pl.BlockSpec((rows, tb), tok)] * 4
                   + [pl.BlockSpec((PEER_HEADS, tb), tok)]),
        out_shape=([jax.ShapeDtypeStruct((d, n), BF16)] + [jax.ShapeDtypeStruct((rows, n), F32)] * 4
                   + [jax.ShapeDtypeStruct((PEER_HEADS, n), F32)]),
        scratch_shapes=[
            pltpu.VMEM((tb, dq), BF16),
            pltpu.VMEM((2, rows, tb), F32),
            pltpu.VMEM((2, TOPK * PEER_HEADS, tb), F32),
        ],
        compiler_params=_params("parallel"),
        name="peer_route",
    )(xn, wq, k1, k2)


def _peer_kernel(xt_ref, s2_ref, e2_ref, s1_ref, e1_ref, tau_ref, u_ref, vt_ref, o_ref, h0, h1, a0, a1, *, nblk):
    eb, tb = h0.shape
    d = o_ref.shape[0]
    n_i1 = eb // N_KEYS
    rb = 2 * SUBLANES
    n_lb = tb // LANES
    hrows = eb // n_lb
    orows = d // n_lb
    s = pl.program_id(0)

    @pl.when(s == 0)
    def _():
        for ref in (h0, h1, a0, a1):
            ref[...] = jnp.zeros_like(ref)

    @pl.when((s < 2) | ((s - 2) % nblk == 0))
    def _():
        o_ref[...] = jnp.zeros_like(o_ref)

    def step_part(h_new, h_prev, a_prev, a_old, lb, carry):
        hr = pl.ds(pl.multiple_of(lb * hrows, hrows), hrows)
        h_new[hr, :] = jnp.dot(u_ref[hr, :], xt_ref[...], preferred_element_type=F32)

        cols = pl.ds(pl.multiple_of(lb * LANES, LANES), LANES)
        for r0 in range(0, N_KEYS, rb):
            gate = [None] * n_i1
            for h in range(PEER_HEADS):
                s2 = s2_ref[h, r0:r0 + rb, cols]
                e2 = e2_ref[h, r0:r0 + rb, cols]
                tau = tau_ref[h:h + 1, cols]
                for a in range(n_i1):
                    sel = (s1_ref[h, a:a + 1, cols] + s2) >= tau
                    t = jnp.where(sel, e2, 0.0) * e1_ref[h, a:a + 1, cols]
                    gate[a] = t if gate[a] is None else gate[a] + t
            for a in range(n_i1):
                hid = h_prev[a * N_KEYS + r0:a * N_KEYS + r0 + rb, cols]
                act = 0.5 * hid * (1.0 + lax.erf(hid * (1.0 / math.sqrt(2.0))))
                a_prev[a * N_KEYS + r0:a * N_KEYS + r0 + rb, cols] = (act * gate[a]).astype(BF16)

        orow = pl.ds(pl.multiple_of(lb * orows, orows), orows)
        o_ref[orow, :] += jnp.dot(vt_ref[orow, :], a_old[...], preferred_element_type=F32)
        return carry

    @pl.when(s % 2 == 0)
    def _():
        lax.fori_loop(0, n_lb, functools.partial(step_part, h0, h1, a1, a0), 0)

    @pl.when(s % 2 == 1)
    def _():
        lax.fori_loop(0, n_lb, functools.partial(step_part, h1, h0, a0, a1), 0)


def _peer(xt, s2, e2, s1, e1, tau, u, vt):
    d, n = xt.shape
    n_exp = u.shape[0]
    tb = _tile(n, 512)
    eb = 1024
    n_i1 = eb // N_KEYS
    nblk = n_exp // eb
    n_tiles = n // tb
    last = n_tiles * nblk - 1
    r3 = lambda a: a.reshape(PEER_HEADS, N_KEYS, n)
    step = lambda s, lag: jnp.clip(s - lag, 0, last)
    tile = lambda s, lag: step(s, lag) // nblk
    blk = lambda s, lag: step(s, lag) % nblk
    return pl.pallas_call(
        functools.partial(_peer_kernel, nblk=nblk),
        grid=(n_tiles * nblk + 2,),
        in_specs=[
            pl.BlockSpec((d, tb), lambda s: (0, tile(s, 0))),
            pl.BlockSpec((PEER_HEADS, N_KEYS, tb), lambda s: (0, 0, tile(s, 1))),
            pl.BlockSpec((PEER_HEADS, N_KEYS, tb), lambda s: (0, 0, tile(s, 1))),
            pl.BlockSpec((PEER_HEADS, n_i1, tb), lambda s: (0, blk(s, 1), tile(s, 1))),
            pl.BlockSpec((PEER_HEADS, n_i1, tb), lambda s: (0, blk(s, 1), tile(s, 1))),
            pl.BlockSpec((PEER_HEADS, tb), lambda s: (0, tile(s, 1))),
            pl.BlockSpec((eb, d), lambda s: (blk(s, 0), 0)),
            pl.BlockSpec((d, eb), lambda s: (0, blk(s, 2))),
        ],
        out_specs=pl.BlockSpec((d, tb), lambda s: (0, tile(s, 2))),
        out_shape=jax.ShapeDtypeStruct((d, n), F32),
        scratch_shapes=[pltpu.VMEM((eb, tb), F32)] * 2 + [pltpu.VMEM((eb, tb), BF16)] * 2,
        compiler_params=_params("arbitrary"),
        name="peer_dense",
    )(xt, r3(s2), r3(e2), r3(s1), r3(e1), tau, u, vt)


def _final_kernel(h_ref, pt_ref, g_ref, y_ref, *, normalize):
    y = h_ref[...] + pt_ref[...].T
    if normalize:
        r = lax.rsqrt(jnp.mean(y * y, axis=-1, keepdims=True) + EPS)
        y = y * r * g_ref[...]
    y_ref[...] = y


def _final(h, pt, g, normalize):
    n, d = h.shape
    tm = _tile(n, 512)
    return pl.pallas_call(
        functools.partial(_final_kernel, normalize=normalize),
        grid=(n // tm,),
        in_specs=[
            pl.BlockSpec((tm, d), lambda i: (i, 0)),
            pl.BlockSpec((d, tm), lambda i: (0, i)),
            pl.BlockSpec((1, d), lambda i: (0, 0)),
        ],
        out_specs=pl.BlockSpec((tm, d), lambda i: (i, 0)),
        out_shape=jax.ShapeDtypeStruct((n, d), F32),
        compiler_params=_params("parallel"),
        name="final",
    )(h, pt, g.reshape(1, d))


def _layer(x, seg_len, prev_k, prev_v, prev_conv, w, norm_final_g, last):
    n, d = x.shape
    n_seg = n // seg_len
    d_q = N_HEADS * HEAD_DIM
    d_kv = N_KV_HEADS * HEAD_DIM
    d_conv = w["conv_w"].shape[1]
    proj = _inproj(x, w["norm_mix_g"], w["w_in"])
    k_col, v_col = d_q // d_kv, d_q // d_kv + 1
    o_b = d_q + 2 * d_kv
    b_col, c_col, x_col = o_b // 512, (o_b + d_conv) // 512, (o_b + 2 * d_conv) // 512
    ga_col, gb_col = (o_b + 3 * d_conv) // 512, (o_b + 3 * d_conv + d) // 512
    assert o_b % 512 == 0 and d_conv % 512 == 0 and d % 512 == 0 and d_q % d_kv == 0

    if prev_k is None:
        tt = _tile(seg_len, 512)
        assert tt % WINDOW == 0
        tps = seg_len // tt
        halo = lambda col: (lambda i: (jnp.maximum(i * (tt // WINDOW) - 1, 0), col))
        attn = _attention(w["attn_sinks"], proj, 0, proj, k_col, proj, v_col,
                          proj, halo(k_col), proj, halo(v_col), n, tt, tps)
        tc = tt
        chalo = lambda col: (lambda i, j: (jnp.maximum(i * (tc // SUBLANES) - 1, 0), col + j))
        bc, utail = _conv(proj, b_col, c_col, x_col, proj, chalo(c_col), proj, chalo(x_col),
                          w["conv_w"], w["conv_b"], tc, seg_len // tc)
        new_k = proj.reshape(n_seg, seg_len, -1)[:, -WINDOW:, d_q:d_q + d_kv]
        new_v = proj.reshape(n_seg, seg_len, -1)[:, -WINDOW:, d_q + d_kv:d_q + 2 * d_kv]
    else:
        assert seg_len == CHUNK
        pk = prev_k.reshape(n_seg * WINDOW, d_kv)
        pv = prev_v.reshape(n_seg * WINDOW, d_kv)
        seg = lambda i: (i, 0)
        attn = _attention(w["attn_sinks"], proj, 0, proj, k_col, proj, v_col,
                          pk, seg, pv, seg, n, seg_len, 0)
        hist = jnp.pad(prev_conv, ((0, 0), (SUBLANES - (CONV_W - 1), 0), (0, 0))).reshape(n_seg * SUBLANES, d_conv)
        hmap = lambda i, j: (i, j)
        bc, utail = _conv(proj, b_col, c_col, x_col, hist, hmap, jnp.ones_like(hist), hmap,
                          w["conv_w"], w["conv_b"], seg_len, 0)
        k_new = proj[:, d_q:d_q + d_kv].reshape(n_seg, seg_len, d_kv)
        v_new = proj[:, d_q + d_kv:d_q + 2 * d_kv].reshape(n_seg, seg_len, d_kv)
        new_k = jnp.concatenate([prev_k.reshape(n_seg, WINDOW, d_kv), k_new], axis=1)[:, -WINDOW:]
        new_v = jnp.concatenate([prev_v.reshape(n_seg, WINDOW, d_kv), v_new], axis=1)[:, -WINDOW:]
    new_conv = utail.reshape(n_seg, -1, SUBLANES, d_conv)[:, -1, SUBLANES - (CONV_W - 1):]

    merged = _merge(attn, bc, w["w_proj_a"], w["w_proj_b"], proj, ga_col, gb_col)
    h, xn = _outproj(merged, x, w["w_out"], w["norm_ffn_g"])
    xt, s2, e2, s1, e1, tau = _route(xn, w["peer_w_query"], w["peer_keys1"], w["peer_keys2"])
    pt = _peer(xt, s2, e2, s1, e1, tau, w["peer_u"], w["peer_vt"])
    y = _final(h, pt, norm_final_g, last)
    shape5 = (n_seg, WINDOW, N_KV_HEADS, HEAD_DIM)
    return y, new_k.reshape(shape5), new_v.reshape(shape5), new_conv


def kernel(x_prompt, x_sample, state_attn_k, state_attn_v, state_conv, norm_mix_g, w_in, attn_sinks, conv_w,
           conv_b, w_proj_a, w_proj_b, w_out, norm_ffn_g, peer_w_query, peer_keys1, peer_keys2, peer_u, peer_v,
           norm_final_g):
    depth = w_in.shape[0]
    bp, sp, d = x_prompt.shape
    bs, ss, _ = x_sample.shape
    yp = x_prompt.reshape(bp * sp, d)
    ys = x_sample.reshape(bs * ss, d)
    outs = [[] for _ in range(6)]
    for l in range(depth):
        w = dict(
            norm_mix_g=norm_mix_g[l], w_in=w_in[l].astype(BF16), attn_sinks=attn_sinks[l],
            conv_w=conv_w[l], conv_b=conv_b[l], w_proj_a=w_proj_a[l].astype(BF16),
            w_proj_b=w_proj_b[l].astype(BF16), w_out=w_out[l].astype(BF16), norm_ffn_g=norm_ffn_g[l],
            peer_w_query=peer_w_query[l].astype(BF16), peer_keys1=peer_keys1[l].astype(BF16),
            peer_keys2=peer_keys2[l].astype(BF16), peer_u=peer_u[l].astype(BF16),
            peer_vt=peer_v[l].astype(BF16).T,
        )
        last = l == depth - 1
        yp, k1, v1, c1 = _layer(yp, sp, None, None, None, w, norm_final_g, last)
        ys, k2, v2, c2 = _layer(ys, ss, state_attn_k[l], state_attn_v[l], state_conv[l], w, norm_final_g, last)
        for lst, val in zip(outs, (k1, v1, c1, k2, v2, c2)):
            lst.append(val)
    return (yp.reshape(bp, sp, d), ys.reshape(bs, ss, d)) + tuple(jnp.stack(o) for o in outs)
```

```python
import functools
import math

import jax
import jax.numpy as jnp
from jax import lax
from jax.experimental import pallas as pl
from jax.experimental.pallas import tpu as pltpu

F32 = jnp.float32
BF16 = jnp.bfloat16

CHUNK = 64
N_HEADS = 32
N_KV_HEADS = 4
GROUP = N_HEADS // N_KV_HEADS
HEAD_DIM = 64
WINDOW = 128
BAND = WINDOW + CHUNK
CONV_W = 3
PEER_HEADS = 8
N_KEYS = 128
D_HALF = 128
TOPK = 16
EPS = 1e-6
NEG_INF = -1e30

SUBLANES = 8
LANES = 128
VMEM_LIMIT = 56 * 1024 * 1024

_NT = (((1,), (1,)), ((), ()))


def _params(*sem, flags=None):
    return pltpu.CompilerParams(dimension_semantics=sem, vmem_limit_bytes=VMEM_LIMIT, flags=flags)


def _tile(n, pref):
    t = min(n, pref)
    while n % t:
        t //= 2
    return t


def _oddeven_merge_sort(n):
    pairs = []

    def merge(lo, m, r):
        step = r * 2
        if step < m:
            merge(lo, m, step)
            merge(lo + r, m, step)
            for i in range(lo + r, lo + m - r, step):
                pairs.append((i, i + r))
        else:
            pairs.append((lo, lo + r))

    def sort(lo, m):
        if m > 1:
            half = m // 2
            sort(lo, half)
            sort(lo + half, half)
            merge(lo, m, 1)

    sort(0, n)
    return pairs


def _bitonic_merge(n):
    pairs = []
    d = n // 2
    while d >= 1:
        for i in range(n):
            if (i & d) == 0:
                pairs.append((i, i + d))
        d //= 2
    return pairs


_SORT16 = _oddeven_merge_sort(TOPK)
_BITONIC16 = _bitonic_merge(TOPK)


def _apply_net(pairs, xs):
    xs = list(xs)
    for i, j in pairs:
        a, b = xs[i], xs[j]
        xs[i] = jnp.maximum(a, b)
        xs[j] = jnp.minimum(a, b)
    return xs


def _top16_bitonic(a, b):
    return [jnp.maximum(a[i], b[TOPK - 1 - i]) for i in range(TOPK)]


def _inproj_kernel(x_ref, g_ref, w_ref, o_ref, xn_ref):
    @pl.when(pl.program_id(1) == 0)
    def _():
        x = x_ref[...]
        r = lax.rsqrt(jnp.mean(x * x, axis=-1, keepdims=True) + EPS)
        xn_ref[...] = (x * r * g_ref[...]).astype(BF16)

    o_ref[...] = jnp.dot(xn_ref[...], w_ref[...], preferred_element_type=F32)


def _inproj(x, g, w):
    n, d = x.shape
    d_in = w.shape[1]
    tm = _tile(n, 1024)
    tn = 512
    return pl.pallas_call(
        _inproj_kernel,
        grid=(n // tm, d_in // tn),
        in_specs=[
            pl.BlockSpec((tm, d), lambda i, j: (i, 0)),
            pl.BlockSpec((1, d), lambda i, j: (0, 0)),
            pl.BlockSpec((d, tn), lambda i, j: (0, j)),
        ],
        out_specs=pl.BlockSpec((tm, tn), lambda i, j: (i, j)),
        out_shape=jax.ShapeDtypeStruct((n, d_in), F32),
        scratch_shapes=[pltpu.VMEM((tm, d), BF16)],
        compiler_params=_params("parallel", "arbitrary"),
        name="inproj",
    )(x, g.reshape(1, d), w)


def _attn_kernel(sink_ref, q_ref, k_ref, v_ref, pk_ref, pv_ref, o_ref, kall, vall, *,
                 n_chunks, tiles_per_seg):
    kall[0:WINDOW, :] = pk_ref[...].astype(BF16)
    vall[0:WINDOW, :] = pv_ref[...].astype(BF16)
    kall[WINDOW:, :] = k_ref[...].astype(BF16)
    vall[WINDOW:, :] = v_ref[...].astype(BF16)
    nq = GROUP * CHUNK
    if tiles_per_seg:
        first_chunk = (pl.program_id(0) % tiles_per_seg) * n_chunks
        band_chunk = lax.broadcasted_iota(jnp.int32, (BAND, nq), 0) // CHUNK
    q_group = lax.broadcasted_iota(jnp.int32, (1, nq), 1) // CHUNK
    sink_rows = []
    for kh in range(N_KV_HEADS):
        row = jnp.zeros((1, nq), F32)
        for g in range(GROUP):
            row = jnp.where(q_group == g, sink_ref[kh * GROUP + g], row)
        sink_rows.append(row)

    def chunk_body(c, carry):
        r0 = pl.multiple_of(c * CHUNK, CHUNK)
        qc = q_ref[pl.ds(r0, CHUNK), :]
        kb = kall[pl.ds(r0, BAND), :]
        vb = vall[pl.ds(r0, BAND), :]
        if tiles_per_seg:
            valid = (first_chunk + c + band_chunk) >= (WINDOW // CHUNK)
        outs = []
        for kh in range(N_KV_HEADS):
            k_h = kb[:, kh * HEAD_DIM:(kh + 1) * HEAD_DIM]
            v_h = vb[:, kh * HEAD_DIM:(kh + 1) * HEAD_DIM]
            q_h = jnp.concatenate(
                [qc[:, (kh * GROUP + g) * HEAD_DIM:(kh * GROUP + g + 1) * HEAD_DIM] for g in range(GROUP)],
                axis=0).astype(BF16)
            s = lax.dot_general(k_h, q_h, _NT, preferred_element_type=F32) * (HEAD_DIM ** -0.5)
            if tiles_per_seg:
                s = jnp.where(valid, s, NEG_INF)
            sink = sink_rows[kh]
            m = jnp.maximum(jnp.max(s, axis=0, keepdims=True), sink)
            p = jnp.exp(s - m)
            probs = p / (jnp.sum(p, axis=0, keepdims=True) + jnp.exp(sink - m))
            o_t = lax.dot_general(v_h, probs.astype(BF16), (((0,), (0,)), ((), ())),
                                  preferred_element_type=F32)
            o = o_t.T
            outs += [o[g * CHUNK:(g + 1) * CHUNK, :] for g in range(GROUP)]
        o_ref[pl.ds(r0, CHUNK), :] = jnp.concatenate(outs, axis=1).astype(BF16)
        return carry

    lax.fori_loop(0, n_chunks, chunk_body, 0)


def _attention(sinks, q_arr, q_col, k_arr, k_col, v_arr, v_col, pk_arr, pk_map, pv_arr, pv_map,
               n, tt, tiles_per_seg):
    d_q = N_HEADS * HEAD_DIM
    d_kv = N_KV_HEADS * HEAD_DIM
    body = functools.partial(_attn_kernel, n_chunks=tt // CHUNK, tiles_per_seg=tiles_per_seg)
    return pl.pallas_call(
        body,
        grid=(n // tt,),
        in_specs=[
            pl.BlockSpec(memory_space=pltpu.SMEM),
            pl.BlockSpec((tt, d_q), lambda i: (i, q_col)),
            pl.BlockSpec((tt, d_kv), lambda i: (i, k_col)),
            pl.BlockSpec((tt, d_kv), lambda i: (i, v_col)),
            pl.BlockSpec((WINDOW, d_kv), pk_map),
            pl.BlockSpec((WINDOW, d_kv), pv_map),
        ],
        out_specs=pl.BlockSpec((tt, d_q), lambda i: (i, 0)),
        out_shape=jax.ShapeDtypeStruct((n, d_q), BF16),
        scratch_shapes=[pltpu.VMEM((WINDOW + tt, d_kv), BF16), pltpu.VMEM((WINDOW + tt, d_kv), BF16)],
        compiler_params=_params("arbitrary"),
        name="attention",
    )(sinks, q_arr, k_arr, v_arr, pk_arr, pv_arr)


def _conv_kernel(b_ref, c_ref, x_ref, hc_ref, hx_ref, w_ref, cb_ref, bc_ref, ut_ref, *, tiles_per_seg):
    u = c_ref[...] * x_ref[...]
    uh = hc_ref[...] * hx_ref[...]
    if tiles_per_seg:
        uh = jnp.where(pl.program_id(0) % tiles_per_seg == 0, 0.0, uh)
    row = lax.broadcasted_iota(jnp.int32, u.shape, 0)
    um1 = jnp.where(row == 0, uh[7:8, :], pltpu.roll(u, 1, axis=0))
    um2 = jnp.where(row == 0, uh[6:7, :], jnp.where(row == 1, uh[7:8, :], pltpu.roll(u, 2, axis=0)))
    conv = cb_ref[...] + w_ref[0:1, :] * um2
    conv = conv + w_ref[1:2, :] * um1
    conv = conv + w_ref[2:3, :] * u
    bc_ref[...] = (b_ref[...] * conv).astype(BF16)
    ut_ref[...] = u[u.shape[0] - SUBLANES:, :]


def _conv(proj, b_col, c_col, x_col, hc_arr, hc_map, hx_arr, hx_map, conv_w, conv_b, tm, tiles_per_seg):
    n = proj.shape[0]
    d_conv = conv_w.shape[1]
    tn = 512
    nh = d_conv // tn
    body = functools.partial(_conv_kernel, tiles_per_seg=tiles_per_seg)
    return pl.pallas_call(
        body,
        grid=(n // tm, nh),
        in_specs=[
            pl.BlockSpec((tm, tn), lambda i, j: (i, b_col + j)),
            pl.BlockSpec((tm, tn), lambda i, j: (i, c_col + j)),
            pl.BlockSpec((tm, tn), lambda i, j: (i, x_col + j)),
            pl.BlockSpec((SUBLANES, tn), hc_map),
            pl.BlockSpec((SUBLANES, tn), hx_map),
            pl.BlockSpec((CONV_W, tn), lambda i, j: (0, j)),
            pl.BlockSpec((1, tn), lambda i, j: (0, j)),
        ],
        out_specs=[
            pl.BlockSpec((tm, tn), lambda i, j: (i, j)),
            pl.BlockSpec((SUBLANES, tn), lambda i, j: (i, j)),
        ],
        out_shape=[
            jax.ShapeDtypeStruct((n, d_conv), BF16),
            jax.ShapeDtypeStruct((n // tm * SUBLANES, d_conv), F32),
        ],
        compiler_params=_params("arbitrary", "arbitrary"),
        name="conv",
    )(proj, proj, proj, hc_arr, hx_arr, conv_w, conv_b.reshape(1, d_conv))


def _merge_kernel(a_ref, bc_ref, wa_ref, wb_ref, ga_ref, gb_ref, o_ref):
    ya = jnp.dot(a_ref[...], wa_ref[...], preferred_element_type=F32)
    yb = jnp.dot(bc_ref[...], wb_ref[...], preferred_element_type=F32)
    o_ref[...] = (jax.nn.sigmoid(ga_ref[...]) * ya + jax.nn.sigmoid(gb_ref[...]) * yb).astype(BF16)


def _merge(attn, bc, wa, wb, proj, ga_col, gb_col):
    n, d_q = attn.shape
    d_conv = bc.shape[1]
    d = wa.shape[1]
    tm = _tile(n, 1024)
    tn = 512
    return pl.pallas_call(
        _merge_kernel,
        grid=(n // tm, d // tn),
        in_specs=[
            pl.BlockSpec((tm, d_q), lambda i, j: (i, 0)),
            pl.BlockSpec((tm, d_conv), lambda i, j: (i, 0)),
            pl.BlockSpec((d_q, tn), lambda i, j: (0, j)),
            pl.BlockSpec((d_conv, tn), lambda i, j: (0, j)),
            pl.BlockSpec((tm, tn), lambda i, j: (i, ga_col + j)),
            pl.BlockSpec((tm, tn), lambda i, j: (i, gb_col + j)),
        ],
        out_specs=pl.BlockSpec((tm, tn), lambda i, j: (i, j)),
        out_shape=jax.ShapeDtypeStruct((n, d), BF16),
        compiler_params=_params("parallel", "arbitrary"),
        name="merge",
    )(attn, bc, wa, wb, proj, proj)


def _outproj_kernel(m_ref, x_ref, w_ref, g_ref, h_ref, xn_ref):
    h = x_ref[...] + jnp.dot(m_ref[...], w_ref[...], preferred_element_type=F32)
    h_ref[...] = h
    r = lax.rsqrt(jnp.mean(h * h, axis=-1, keepdims=True) + EPS)
    xn_ref[...] = (h * r * g_ref[...]).astype(BF16)


def _outproj(merged, x, w, g):
    n, d = x.shape
    tm = _tile(n, 512)
    return pl.pallas_call(
        _outproj_kernel,
        grid=(n // tm,),
        in_specs=[
            pl.BlockSpec((tm, d), lambda i: (i, 0)),
            pl.BlockSpec((tm, d), lambda i: (i, 0)),
            pl.BlockSpec((d, d), lambda i: (0, 0)),
            pl.BlockSpec((1, d), lambda i: (0, 0)),
        ],
        out_specs=[pl.BlockSpec((tm, d), lambda i: (i, 0)), pl.BlockSpec((tm, d), lambda i: (i, 0))],
        out_shape=[jax.ShapeDtypeStruct((n, d), F32), jax.ShapeDtypeStruct((n, d), BF16)],
        compiler_params=_params("parallel"),
        name="outproj",
    )(merged, x, w, g.reshape(1, d))


def _pair_threshold(v1, v2):
    c = lambda i, j: v1[i] + v2[j]
    row0 = [c(0, j) for j in range(TOPK)]
    grp1 = [c(1, j) for j in range(8)] + [c(2, j) for j in range(5)] + [c(3, j) for j in range(3)]
    grp2 = ([c(3, 3)] + [c(4, j) for j in range(3)] + [c(5, 0), c(5, 1), c(6, 0), c(6, 1), c(7, 0), c(7, 1)]
            + [c(i, 0) for i in range(8, 14)])
    top = _apply_net(_BITONIC16, _top16_bitonic(row0, _apply_net(_SORT16, grp1)))
    top = _apply_net(_BITONIC16, _top16_bitonic(top, _apply_net(_SORT16, grp2)))
    top[TOPK - 1] = jnp.maximum(top[TOPK - 1], c(14, 0))
    top[TOPK - 2] = jnp.maximum(top[TOPK - 2], c(15, 0))
    tau = functools.reduce(jnp.minimum, top)
    m = row0[0]
    z = functools.reduce(lambda a, b: a + b, [jnp.exp(t - m) for t in top])
    return tau, z


def _route_kernel(xn_ref, wq_ref, k1_ref, k2_ref, xt_ref, s2_ref, e2_ref, s1_ref, e1_ref, tau_ref,
                  q_scr, s_scr, top_scr):
    tb = xn_ref.shape[0]
    q_scr[...] = jnp.dot(xn_ref[...], wq_ref[...], preferred_element_type=F32).astype(BF16)
    xt_ref[...] = xn_ref[...].astype(F32).T.astype(BF16)
    for h in range(PEER_HEADS):
        rows = slice(h * N_KEYS, (h + 1) * N_KEYS)
        q1 = q_scr[:, 2 * h * D_HALF:(2 * h + 1) * D_HALF]
        q2 = q_scr[:, (2 * h + 1) * D_HALF:(2 * h + 2) * D_HALF]
        s_scr[0, rows, :] = lax.dot_general(k1_ref[h], q1, _NT, preferred_element_type=F32)
        s_scr[1, rows, :] = lax.dot_general(k2_ref[h], q2, _NT, preferred_element_type=F32)

    def lane_block(lb, carry):
        cols = pl.ds(pl.multiple_of(lb * LANES, LANES), LANES)
        for side in range(2):
            for h in range(PEER_HEADS):
                lst = [s_scr[side, pl.ds(h * N_KEYS + SUBLANES * r, SUBLANES), cols]
                       for r in range(N_KEYS // SUBLANES)]
                lst = _apply_net(_SORT16, lst)
                for d in (4, 2, 1):
                    other = [pltpu.roll(x, d, axis=0) for x in lst]
                    lst = _apply_net(_BITONIC16, _top16_bitonic(lst, other))
                for i in range(TOPK):
                    top_scr[side, pl.ds(i * PEER_HEADS + h, 1), cols] = lst[i][0:1, :]
        v1 = [top_scr[0, pl.ds(i * PEER_HEADS, PEER_HEADS), cols] for i in range(TOPK)]
        v2 = [top_scr[1, pl.ds(i * PEER_HEADS, PEER_HEADS), cols] for i in range(TOPK)]
        tau, z = _pair_threshold(v1, v2)
        tau_ref[:, cols] = tau
        for h in range(PEER_HEADS):
            rows = slice(h * N_KEYS, (h + 1) * N_KEYS)
            s1 = s_scr[0, rows, cols]
            s2 = s_scr[1, rows, cols]
            s1_ref[rows, cols] = s1
            e1_ref[rows, cols] = jnp.exp(s1 - v1[0][h:h + 1, :])
            s2_ref[rows, cols] = s2
            e2_ref[rows, cols] = jnp.exp(s2 - v2[0][h:h + 1, :]) / z[h:h + 1, :]
        return carry

    lax.fori_loop(0, tb // LANES, lane_block, 0)


def _route(xn, wq, k1, k2):
    n, d = xn.shape
    dq = wq.shape[1]
    tb = _tile(n, 512)
    rows = PEER_HEADS * N_KEYS
    tok = lambda i: (0, i)
    return pl.pallas_call(
        _route_kernel,
        grid=(n // tb,),
        in_specs=[
            pl.BlockSpec((tb, d), lambda i: (i, 0)),
            pl.BlockSpec((d, dq), lambda i: (0, 0)),
            pl.BlockSpec((PEER_HEADS, N_KEYS, D_HALF), lambda i: (0, 0, 0)),
            pl.BlockSpec((PEER_HEADS, N_KEYS, D_HALF), lambda i: (0, 0, 0)),
        ],
        out_specs=([pl.BlockSpec((d, tb), tok)] + [pl.BlockSpec((rows, tb), tok)] * 4
                   + [pl.BlockSpec((PEER_HEADS, tb), tok)]),
        out_shape=([jax.ShapeDtypeStruct((d, n), BF16)] + [jax.ShapeDtypeStruct((rows, n), F32)] * 4
                   + [jax.ShapeDtypeStruct((PEER_HEADS, n), F32)]),
        scratch_shapes=[
            pltpu.VMEM((tb, dq), BF16),
            pltpu.VMEM((2, rows, tb), F32),
            pltpu.VMEM((2, TOPK * PEER_HEADS, tb), F32),
        ],
        compiler_params=_params("parallel"),
        name="peer_route",
    )(xn, wq, k1, k2)


def _peer_kernel(xt_ref, s2_ref, e2_ref, s1_ref, e1_ref, tau_ref, u_ref, vt_ref, o_ref, h0, h1, a0, a1, *, nblk):
    eb, tb = h0.shape
    d = o_ref.shape[0]
    n_i1 = eb // N_KEYS
    rb = 2 * SUBLANES
    n_lb = tb // LANES
    hrows = eb // n_lb
    orows = d // n_lb
    s = pl.program_id(0)

    @pl.when(s == 0)
    def _():
        for ref in (h0, h1, a0, a1):
            ref[...] = jnp.zeros_like(ref)

    @pl.when((s < 2) | ((s - 2) % nblk == 0))
    def _():
        o_ref[...] = jnp.zeros_like(o_ref)

    def step_part(h_new, h_prev, a_prev, a_old, lb):
        hr = pl.ds(lb * hrows, hrows)
        h_new[hr, :] = jnp.dot(u_ref[hr, :], xt_ref[...], preferred_element_type=F32)

        cols = pl.ds(lb * LANES, LANES)
        for r0 in range(0, N_KEYS, rb):
            gate = [None] * n_i1
            for h in range(PEER_HEADS):
                s2 = s2_ref[h, r0:r0 + rb, cols]
                e2 = e2_ref[h, r0:r0 + rb, cols]
                tau = tau_ref[h:h + 1, cols]
                for a in range(n_i1):
                    sel = (s1_ref[h, a:a + 1, cols] + s2) >= tau
                    t = jnp.where(sel, e2, 0.0) * e1_ref[h, a:a + 1, cols]
                    gate[a] = t if gate[a] is None else gate[a] + t
            for a in range(n_i1):
                hid = h_prev[a * N_KEYS + r0:a * N_KEYS + r0 + rb, cols]
                act = 0.5 * hid * (1.0 + lax.erf(hid * (1.0 / math.sqrt(2.0))))
                a_prev[a * N_KEYS + r0:a * N_KEYS + r0 + rb, cols] = (act * gate[a]).astype(BF16)

        orow = pl.ds(lb * orows, orows)
        o_ref[orow, :] += jnp.dot(vt_ref[orow, :], a_old[...], preferred_element_type=F32)

    @pl.when(s % 2 == 0)
    def _():
        for lb in range(n_lb):
            step_part(h0, h1, a1, a0, lb)

    @pl.when(s % 2 == 1)
    def _():
        for lb in range(n_lb):
            step_part(h1, h0, a0, a1, lb)


def _peer(xt, s2, e2, s1, e1, tau, u, vt):
    d, n = xt.shape
    n_exp = u.shape[0]
    tb = _tile(n, 512)
    eb = 1024
    n_i1 = eb // N_KEYS
    nblk = n_exp // eb
    n_tiles = n // tb
    last = n_tiles * nblk - 1
    r3 = lambda a: a.reshape(PEER_HEADS, N_KEYS, n)
    step = lambda s, lag: jnp.clip(s - lag, 0, last)
    tile = lambda s, lag: step(s, lag) // nblk
    blk = lambda s, lag: step(s, lag) % nblk
    return pl.pallas_call(
        functools.partial(_peer_kernel, nblk=nblk),
        grid=(n_tiles * nblk + 2,),
        in_specs=[
            pl.BlockSpec((d, tb), lambda s: (0, tile(s, 0))),
            pl.BlockSpec((PEER_HEADS, N_KEYS, tb), lambda s: (0, 0, tile(s, 1))),
            pl.BlockSpec((PEER_HEADS, N_KEYS, tb), lambda s: (0, 0, tile(s, 1))),
            pl.BlockSpec((PEER_HEADS, n_i1, tb), lambda s: (0, blk(s, 1), tile(s, 1))),
            pl.BlockSpec((PEER_HEADS, n_i1, tb), lambda s: (0, blk(s, 1), tile(s, 1))),
            pl.BlockSpec((PEER_HEADS, tb), lambda s: (0, tile(s, 1))),
            pl.BlockSpec((eb, d), lambda s: (blk(s, 0), 0)),
            pl.BlockSpec((d, eb), lambda s: (0, blk(s, 2))),
        ],
        out_specs=pl.BlockSpec((d, tb), lambda s: (0, tile(s, 2))),
        out_shape=jax.ShapeDtypeStruct((d, n), F32),
        scratch_shapes=[pltpu.VMEM((eb, tb), F32)] * 2 + [pltpu.VMEM((eb, tb), BF16)] * 2,
        compiler_params=_params("arbitrary"),
        name="peer_dense",
    )(xt, r3(s2), r3(e2), r3(s1), r3(e1), tau, u, vt)


def _final_kernel(h_ref, pt_ref, g_ref, y_ref, *, normalize):
    y = h_ref[...] + pt_ref[...].T
    if normalize:
        r = lax.rsqrt(jnp.mean(y * y, axis=-1, keepdims=True) + EPS)
        y = y * r * g_ref[...]
    y_ref[...] = y


def _final(h, pt, g, normalize):
    n, d = h.shape
    tm = _tile(n, 512)
    return pl.pallas_call(
        functools.partial(_final_kernel, normalize=normalize),
        grid=(n // tm,),
        in_specs=[
            pl.BlockSpec((tm, d), lambda i: (i, 0)),
            pl.BlockSpec((d, tm), lambda i: (0, i)),
            pl.BlockSpec((1, d), lambda i: (0, 0)),
        ],
        out_specs=pl.BlockSpec((tm, d), lambda i: (i, 0)),
        out_shape=jax.ShapeDtypeStruct((n, d), F32),
        compiler_params=_params("parallel"),
        name="final",
    )(h, pt, g.reshape(1, d))


def _layer(x, seg_len, prev_k, prev_v, prev_conv, w, norm_final_g, last):
    n, d = x.shape
    n_seg = n // seg_len
    d_q = N_HEADS * HEAD_DIM
    d_kv = N_KV_HEADS * HEAD_DIM
    d_conv = w["conv_w"].shape[1]
    proj = _inproj(x, w["norm_mix_g"], w["w_in"])
    k_col, v_col = d_q // d_kv, d_q // d_kv + 1
    o_b = d_q + 2 * d_kv
    b_col, c_col, x_col = o_b // 512, (o_b + d_conv) // 512, (o_b + 2 * d_conv) // 512
    ga_col, gb_col = (o_b + 3 * d_conv) // 512, (o_b + 3 * d_conv + d) // 512
    assert o_b % 512 == 0 and d_conv % 512 == 0 and d % 512 == 0 and d_q % d_kv == 0

    if prev_k is None:
        tt = _tile(seg_len, 512)
        assert tt % WINDOW == 0
        tps = seg_len // tt
        halo = lambda col: (lambda i: (jnp.maximum(i * (tt // WINDOW) - 1, 0), col))
        attn = _attention(w["attn_sinks"], proj, 0, proj, k_col, proj, v_col,
                          proj, halo(k_col), proj, halo(v_col), n, tt, tps)
        tc = tt
        chalo = lambda col: (lambda i, j: (jnp.maximum(i * (tc // SUBLANES) - 1, 0), col + j))
        bc, utail = _conv(proj, b_col, c_col, x_col, proj, chalo(c_col), proj, chalo(x_col),
                          w["conv_w"], w["conv_b"], tc, seg_len // tc)
        new_k = proj.reshape(n_seg, seg_len, -1)[:, -WINDOW:, d_q:d_q + d_kv]
        new_v = proj.reshape(n_seg, seg_len, -1)[:, -WINDOW:, d_q + d_kv:d_q + 2 * d_kv]
    else:
        assert seg_len == CHUNK
        pk = prev_k.reshape(n_seg * WINDOW, d_kv)
        pv = prev_v.reshape(n_seg * WINDOW, d_kv)
        seg = lambda i: (i, 0)
        attn = _attention(w["attn_sinks"], proj, 0, proj, k_col, proj, v_col,
                          pk, seg, pv, seg, n, seg_len, 0)
        hist = jnp.pad(prev_conv, ((0, 0), (SUBLANES - (CONV_W - 1), 0), (0, 0))).reshape(n_seg * SUBLANES, d_conv)
        hmap = lambda i, j: (i, j)
        bc, utail = _conv(proj, b_col, c_col, x_col, hist, hmap, jnp.ones_like(hist), hmap,
                          w["conv_w"], w["conv_b"], seg_len, 0)
        k_new = proj[:, d_q:d_q + d_kv].reshape(n_seg, seg_len, d_kv)
        v_new = proj[:, d_q + d_kv:d_q + 2 * d_kv].reshape(n_seg, seg_len, d_kv)
        new_k = jnp.concatenate([prev_k.reshape(n_seg, WINDOW, d_kv), k_new], axis=1)[:, -WINDOW:]
        new_v = jnp.concatenate([prev_v.reshape(n_seg, WINDOW, d_kv), v_new], axis=1)[:, -WINDOW:]
    new_conv = utail.reshape(n_seg, -1, SUBLANES, d_conv)[:, -1, SUBLANES - (CONV_W - 1):]

    merged = _merge(attn, bc, w["w_proj_a"], w["w_proj_b"], proj, ga_col, gb_col)
    h, xn = _outproj(merged, x, w["w_out"], w["norm_ffn_g"])
    xt, s2, e2, s1, e1, tau = _route(xn, w["peer_w_query"], w["peer_keys1"], w["peer_keys2"])
    pt = _peer(xt, s2, e2, s1, e1, tau, w["peer_u"], w["peer_vt"])
    y = _final(h, pt, norm_final_g, last)
    shape5 = (n_seg, WINDOW, N_KV_HEADS, HEAD_DIM)
    return y, new_k.reshape(shape5), new_v.reshape(shape5), new_conv


def kernel(x_prompt, x_sample, state_attn_k, state_attn_v, state_conv, norm_mix_g, w_in, attn_sinks, conv_w,
           conv_b, w_proj_a, w_proj_b, w_out, norm_ffn_g, peer_w_query, peer_keys1, peer_keys2, peer_u, peer_v,
           norm_final_g):
    depth = w_in.shape[0]
    bp, sp, d = x_prompt.shape
    bs, ss, _ = x_sample.shape
    yp = x_prompt.reshape(bp * sp, d)
    ys = x_sample.reshape(bs * ss, d)
    outs = [[] for _ in range(6)]
    for l in range(depth):
        w = dict(
            norm_mix_g=norm_mix_g[l], w_in=w_in[l].astype(BF16), attn_sinks=attn_sinks[l],
            conv_w=conv_w[l], conv_b=conv_b[l], w_proj_a=w_proj_a[l].astype(BF16),
            w_proj_b=w_proj_b[l].astype(BF16), w_out=w_out[l].astype(BF16), norm_ffn_g=norm_ffn_g[l],
            peer_w_query=peer_w_query[l].astype(BF16), peer_keys1=peer_keys1[l].astype(BF16),
            peer_keys2=peer_keys2[l].astype(BF16), peer_u=peer_u[l].astype(BF16),
            peer_vt=peer_v[l].astype(BF16).T,
        )
        last = l == depth - 1
        yp, k1, v1, c1 = _layer(yp, sp, None, None, None, w, norm_final_g, last)
        ys, k2, v2, c2 = _layer(ys, ss, state_attn_k[l], state_attn_v[l], state_conv[l], w, norm_final_g, last)
        for lst, val in zip(outs, (k1, v1, c1, k2, v2, c2)):
            lst.append(val)
    return (yp.reshape(bp, sp, d), ys.reshape(bs, ss, d)) + tuple(jnp.stack(o) for o in outs)
```

```python
import functools
import math

import jax
import jax.numpy as jnp
from jax import lax
from jax.experimental import pallas as pl
from jax.experimental.pallas import tpu as pltpu

F32 = jnp.float32
BF16 = jnp.bfloat16

CHUNK = 64
N_HEADS = 32
N_KV_HEADS = 4
GROUP = N_HEADS // N_KV_HEADS
HEAD_DIM = 64
WINDOW = 128
BAND = WINDOW + CHUNK
CONV_W = 3
PEER_HEADS = 8
N_KEYS = 128
D_HALF = 128
TOPK = 16
EPS = 1e-6
NEG_INF = -1e30

SUBLANES = 8
LANES = 128
VMEM_LIMIT = 56 * 1024 * 1024

_NT = (((1,), (1,)), ((), ()))


def _params(*sem, flags=None):
    return pltpu.CompilerParams(dimension_semantics=sem, vmem_limit_bytes=VMEM_LIMIT, flags=flags)


def _tile(n, pref):
    t = min(n, pref)
    while n % t:
        t //= 2
    return t


def _oddeven_merge_sort(n):
    pairs = []

    def merge(lo, m, r):
        step = r * 2
        if step < m:
            merge(lo, m, step)
            merge(lo + r, m, step)
            for i in range(lo + r, lo + m - r, step):
                pairs.append((i, i + r))
        else:
            pairs.append((lo, lo + r))

    def sort(lo, m):
        if m > 1:
            half = m // 2
            sort(lo, half)
            sort(lo + half, half)
            merge(lo, m, 1)

    sort(0, n)
    return pairs


def _bitonic_merge(n):
    pairs = []
    d = n // 2
    while d >= 1:
        for i in range(n):
            if (i & d) == 0:
                pairs.append((i, i + d))
        d //= 2
    return pairs


_SORT16 = _oddeven_merge_sort(TOPK)
_BITONIC16 = _bitonic_merge(TOPK)


def _apply_net(pairs, xs):
    xs = list(xs)
    for i, j in pairs:
        a, b = xs[i], xs[j]
        xs[i] = jnp.maximum(a, b)
        xs[j] = jnp.minimum(a, b)
    return xs


def _top16_bitonic(a, b):
    return [jnp.maximum(a[i], b[TOPK - 1 - i]) for i in range(TOPK)]


def _inproj_kernel(x_ref, g_ref, w_ref, o_ref, xn_ref):
    @pl.when(pl.program_id(1) == 0)
    def _():
        x = x_ref[...]
        r = lax.rsqrt(jnp.mean(x * x, axis=-1, keepdims=True) + EPS)
        xn_ref[...] = (x * r * g_ref[...]).astype(BF16)

    o_ref[...] = jnp.dot(xn_ref[...], w_ref[...], preferred_element_type=F32)


def _inproj(x, g, w):
    n, d = x.shape
    d_in = w.shape[1]
    tm = _tile(n, 1024)
    tn = 512
    return pl.pallas_call(
        _inproj_kernel,
        grid=(n // tm, d_in // tn),
        in_specs=[
            pl.BlockSpec((tm, d), lambda i, j: (i, 0)),
            pl.BlockSpec((1, d), lambda i, j: (0, 0)),
            pl.BlockSpec((d, tn), lambda i, j: (0, j)),
        ],
        out_specs=pl.BlockSpec((tm, tn), lambda i, j: (i, j)),
        out_shape=jax.ShapeDtypeStruct((n, d_in), F32),
        scratch_shapes=[pltpu.VMEM((tm, d), BF16)],
        compiler_params=_params("parallel", "arbitrary"),
        name="inproj",
    )(x, g.reshape(1, d), w)


def _attn_kernel(sink_ref, q_ref, k_ref, v_ref, pk_ref, pv_ref, o_ref, kall, vall, *,
                 n_chunks, tiles_per_seg):
    kall[0:WINDOW, :] = pk_ref[...].astype(BF16)
    vall[0:WINDOW, :] = pv_ref[...].astype(BF16)
    kall[WINDOW:, :] = k_ref[...].astype(BF16)
    vall[WINDOW:, :] = v_ref[...].astype(BF16)
    nq = GROUP * CHUNK
    if tiles_per_seg:
        first_chunk = (pl.program_id(0) % tiles_per_seg) * n_chunks
        band_chunk = lax.broadcasted_iota(jnp.int32, (BAND, nq), 0) // CHUNK
    q_group = lax.broadcasted_iota(jnp.int32, (1, nq), 1) // CHUNK
    sink_rows = []
    for kh in range(N_KV_HEADS):
        row = jnp.zeros((1, nq), F32)
        for g in range(GROUP):
            row = jnp.where(q_group == g, sink_ref[kh * GROUP + g], row)
        sink_rows.append(row)

    def chunk_body(c, carry):
        r0 = pl.multiple_of(c * CHUNK, CHUNK)
        qc = q_ref[pl.ds(r0, CHUNK), :]
        kb = kall[pl.ds(r0, BAND), :]
        vb = vall[pl.ds(r0, BAND), :]
        if tiles_per_seg:
            valid = (first_chunk + c + band_chunk) >= (WINDOW // CHUNK)
        outs = []
        for kh in range(N_KV_HEADS):
            k_h = kb[:, kh * HEAD_DIM:(kh + 1) * HEAD_DIM]
            v_h = vb[:, kh * HEAD_DIM:(kh + 1) * HEAD_DIM]
            q_h = jnp.concatenate(
                [qc[:, (kh * GROUP + g) * HEAD_DIM:(kh * GROUP + g + 1) * HEAD_DIM] for g in range(GROUP)],
                axis=0).astype(BF16)
            s = lax.dot_general(k_h, q_h, _NT, preferred_element_type=F32) * (HEAD_DIM ** -0.5)
            if tiles_per_seg:
                s = jnp.where(valid, s, NEG_INF)
            sink = sink_rows[kh]
            m = jnp.maximum(jnp.max(s, axis=0, keepdims=True), sink)
            p = jnp.exp(s - m)
            probs = p / (jnp.sum(p, axis=0, keepdims=True) + jnp.exp(sink - m))
            o_t = lax.dot_general(v_h, probs.astype(BF16), (((0,), (0,)), ((), ())),
                                  preferred_element_type=F32)
            o = o_t.T
            outs += [o[g * CHUNK:(g + 1) * CHUNK, :] for g in range(GROUP)]
        o_ref[pl.ds(r0, CHUNK), :] = jnp.concatenate(outs, axis=1).astype(BF16)
        return carry

    lax.fori_loop(0, n_chunks, chunk_body, 0)


def _attention(sinks, q_arr, q_col, k_arr, k_col, v_arr, v_col, pk_arr, pk_map, pv_arr, pv_map,
               n, tt, tiles_per_seg):
    d_q = N_HEADS * HEAD_DIM
    d_kv = N_KV_HEADS * HEAD_DIM
    body = functools.partial(_attn_kernel, n_chunks=tt // CHUNK, tiles_per_seg=tiles_per_seg)
    return pl.pallas_call(
        body,
        grid=(n // tt,),
        in_specs=[
            pl.BlockSpec(memory_space=pltpu.SMEM),
            pl.BlockSpec((tt, d_q), lambda i: (i, q_col)),
            pl.BlockSpec((tt, d_kv), lambda i: (i, k_col)),
            pl.BlockSpec((tt, d_kv), lambda i: (i, v_col)),
            pl.BlockSpec((WINDOW, d_kv), pk_map),
            pl.BlockSpec((WINDOW, d_kv), pv_map),
        ],
        out_specs=pl.BlockSpec((tt, d_q), lambda i: (i, 0)),
        out_shape=jax.ShapeDtypeStruct((n, d_q), BF16),
        scratch_shapes=[pltpu.VMEM((WINDOW + tt, d_kv), BF16), pltpu.VMEM((WINDOW + tt, d_kv), BF16)],
        compiler_params=_params("arbitrary"),
        name="attention",
    )(sinks, q_arr, k_arr, v_arr, pk_arr, pv_arr)


def _conv_kernel(b_ref, c_ref, x_ref, hc_ref, hx_ref, w_ref, cb_ref, bc_ref, ut_ref, *, tiles_per_seg):
    u = c_ref[...] * x_ref[...]
    uh = hc_ref[...] * hx_ref[...]
    if tiles_per_seg:
        uh = jnp.where(pl.program_id(0) % tiles_per_seg == 0, 0.0, uh)
    row = lax.broadcasted_iota(jnp.int32, u.shape, 0)
    um1 = jnp.where(row == 0, uh[7:8, :], pltpu.roll(u, 1, axis=0))
    um2 = jnp.where(row == 0, uh[6:7, :], jnp.where(row == 1, uh[7:8, :], pltpu.roll(u, 2, axis=0)))
    conv = cb_ref[...] + w_ref[0:1, :] * um2
    conv = conv + w_ref[1:2, :] * um1
    conv = conv + w_ref[2:3, :] * u
    bc_ref[...] = (b_ref[...] * conv).astype(BF16)
    ut_ref[...] = u[u.shape[0] - SUBLANES:, :]


def _conv(proj, b_col, c_col, x_col, hc_arr, hc_map, hx_arr, hx_map, conv_w, conv_b, tm, tiles_per_seg):
    n = proj.shape[0]
    d_conv = conv_w.shape[1]
    tn = 512
    nh = d_conv // tn
    body = functools.partial(_conv_kernel, tiles_per_seg=tiles_per_seg)
    return pl.pallas_call(
        body,
        grid=(n // tm, nh),
        in_specs=[
            pl.BlockSpec((tm, tn), lambda i, j: (i, b_col + j)),
            pl.BlockSpec((tm, tn), lambda i, j: (i, c_col + j)),
            pl.BlockSpec((tm, tn), lambda i, j: (i, x_col + j)),
            pl.BlockSpec((SUBLANES, tn), hc_map),
            pl.BlockSpec((SUBLANES, tn), hx_map),
            pl.BlockSpec((CONV_W, tn), lambda i, j: (0, j)),
            pl.BlockSpec((1, tn), lambda i, j: (0, j)),
        ],
        out_specs=[
            pl.BlockSpec((tm, tn), lambda i, j: (i, j)),
            pl.BlockSpec((SUBLANES, tn), lambda i, j: (i, j)),
        ],
        out_shape=[
            jax.ShapeDtypeStruct((n, d_conv), BF16),
            jax.ShapeDtypeStruct((n // tm * SUBLANES, d_conv), F32),
        ],
        compiler_params=_params("arbitrary", "arbitrary"),
        name="conv",
    )(proj, proj, proj, hc_arr, hx_arr, conv_w, conv_b.reshape(1, d_conv))


def _merge_kernel(a_ref, bc_ref, wa_ref, wb_ref, ga_ref, gb_ref, o_ref):
    ya = jnp.dot(a_ref[...], wa_ref[...], preferred_element_type=F32)
    yb = jnp.dot(bc_ref[...], wb_ref[...], preferred_element_type=F32)
    o_ref[...] = (jax.nn.sigmoid(ga_ref[...]) * ya + jax.nn.sigmoid(gb_ref[...]) * yb).astype(BF16)


def _merge(attn, bc, wa, wb, proj, ga_col, gb_col):
    n, d_q = attn.shape
    d_conv = bc.shape[1]
    d = wa.shape[1]
    tm = _tile(n, 1024)
    tn = 512
    return pl.pallas_call(
        _merge_kernel,
        grid=(n // tm, d // tn),
        in_specs=[
            pl.BlockSpec((tm, d_q), lambda i, j: (i, 0)),
            pl.BlockSpec((tm, d_conv), lambda i, j: (i, 0)),
            pl.BlockSpec((d_q, tn), lambda i, j: (0, j)),
            pl.BlockSpec((d_conv, tn), lambda i, j: (0, j)),
            pl.BlockSpec((tm, tn), lambda i, j: (i, ga_col + j)),
            pl.BlockSpec((tm, tn), lambda i, j: (i, gb_col + j)),
        ],
        out_specs=pl.BlockSpec((tm, tn), lambda i, j: (i, j)),
        out_shape=jax.ShapeDtypeStruct((n, d), BF16),
        compiler_params=_params("parallel", "arbitrary"),
        name="merge",
    )(attn, bc, wa, wb, proj, proj)


def _outproj_kernel(m_ref, x_ref, w_ref, g_ref, h_ref, xn_ref):
    h = x_ref[...] + jnp.dot(m_ref[...], w_ref[...], preferred_element_type=F32)
    h_ref[...] = h
    r = lax.rsqrt(jnp.mean(h * h, axis=-1, keepdims=True) + EPS)
    xn_ref[...] = (h * r * g_ref[...]).astype(BF16)


def _outproj(merged, x, w, g):
    n, d = x.shape
    tm = _tile(n, 512)
    return pl.pallas_call(
        _outproj_kernel,
        grid=(n // tm,),
        in_specs=[
            pl.BlockSpec((tm, d), lambda i: (i, 0)),
            pl.BlockSpec((tm, d), lambda i: (i, 0)),
            pl.BlockSpec((d, d), lambda i: (0, 0)),
            pl.BlockSpec((1, d), lambda i: (0, 0)),
        ],
        out_specs=[pl.BlockSpec((tm, d), lambda i: (i, 0)), pl.BlockSpec((tm, d), lambda i: (i, 0))],
        out_shape=[jax.ShapeDtypeStruct((n, d), F32), jax.ShapeDtypeStruct((n, d), BF16)],
        compiler_params=_params("parallel"),
        name="outproj",
    )(merged, x, w, g.reshape(1, d))


def _pair_threshold(v1, v2):
    c = lambda i, j: v1[i] + v2[j]
    row0 = [c(0, j) for j in range(TOPK)]
    grp1 = [c(1, j) for j in range(8)] + [c(2, j) for j in range(5)] + [c(3, j) for j in range(3)]
    grp2 = ([c(3, 3)] + [c(4, j) for j in range(3)] + [c(5, 0), c(5, 1), c(6, 0), c(6, 1), c(7, 0), c(7, 1)]
            + [c(i, 0) for i in range(8, 14)])
    top = _apply_net(_BITONIC16, _top16_bitonic(row0, _apply_net(_SORT16, grp1)))
    top = _apply_net(_BITONIC16, _top16_bitonic(top, _apply_net(_SORT16, grp2)))
    top[TOPK - 1] = jnp.maximum(top[TOPK - 1], c(14, 0))
    top[TOPK - 2] = jnp.maximum(top[TOPK - 2], c(15, 0))
    tau = functools.reduce(jnp.minimum, top)
    m = row0[0]
    z = functools.reduce(lambda a, b: a + b, [jnp.exp(t - m) for t in top])
    return tau, z


def _route_kernel(xn_ref, wq_ref, k1_ref, k2_ref, xt_ref, s2_ref, e2_ref, th_ref, e1_ref,
                  q_scr, s_scr, top_scr):
    tb = xn_ref.shape[0]
    q_scr[...] = jnp.dot(xn_ref[...], wq_ref[...], preferred_element_type=F32).astype(BF16)
    xt_ref[...] = xn_ref[...].astype(F32).T.astype(BF16)
    for h in range(PEER_HEADS):
        rows = slice(h * N_KEYS, (h + 1) * N_KEYS)
        q1 = q_scr[:, 2 * h * D_HALF:(2 * h + 1) * D_HALF]
        q2 = q_scr[:, (2 * h + 1) * D_HALF:(2 * h + 2) * D_HALF]
        s_scr[0, rows, :] = lax.dot_general(k1_ref[h], q1, _NT, preferred_element_type=F32)
        s_scr[1, rows, :] = lax.dot_general(k2_ref[h], q2, _NT, preferred_element_type=F32)

    def lane_block(lb, carry):
        cols = pl.ds(pl.multiple_of(lb * LANES, LANES), LANES)
        for side in range(2):
            for h in range(PEER_HEADS):
                lst = [s_scr[side, pl.ds(h * N_KEYS + SUBLANES * r, SUBLANES), cols]
                       for r in range(N_KEYS // SUBLANES)]
                lst = _apply_net(_SORT16, lst)
                for d in (4, 2, 1):
                    other = [pltpu.roll(x, d, axis=0) for x in lst]
                    lst = _apply_net(_BITONIC16, _top16_bitonic(lst, other))
                for i in range(TOPK):
                    top_scr[side, pl.ds(i * PEER_HEADS + h, 1), cols] = lst[i][0:1, :]
        v1 = [top_scr[0, pl.ds(i * PEER_HEADS, PEER_HEADS), cols] for i in range(TOPK)]
        v2 = [top_scr[1, pl.ds(i * PEER_HEADS, PEER_HEADS), cols] for i in range(TOPK)]
        tau, z = _pair_threshold(v1, v2)
        reach = []
        for j in range(TOPK):
            t = jnp.full_like(tau, jnp.inf)
            for r in range(TOPK // (j + 1)):
                t = jnp.where(v1[r] + v2[j] >= tau, v1[r], t)
            reach.append(t)
        for h in range(PEER_HEADS):
            rows = slice(h * N_KEYS, (h + 1) * N_KEYS)
            s1 = s_scr[0, rows, cols]
            s2 = s_scr[1, rows, cols]
            th = jnp.full_like(s1, jnp.inf)
            for j in range(TOPK):
                th = jnp.where(s1 >= reach[j][h:h + 1, :], v2[j][h:h + 1, :], th)
            th_ref[rows, cols] = th
            e1_ref[rows, cols] = jnp.exp(s1 - v1[0][h:h + 1, :])
            s2_ref[rows, cols] = s2
            e2_ref[rows, cols] = jnp.exp(s2 - v2[0][h:h + 1, :]) / z[h:h + 1, :]
        return carry

    lax.fori_loop(0, tb // LANES, lane_block, 0)


def _route(xn, wq, k1, k2):
    n, d = xn.shape
    dq = wq.shape[1]
    tb = _tile(n, 512)
    rows = PEER_HEADS * N_KEYS
    tok = lambda i: (0, i)
    return pl.pallas_call(
        _route_kernel,
        grid=(n // tb,),
        in_specs=[
            pl.BlockSpec((tb, d), lambda i: (i, 0)),
            pl.BlockSpec((d, dq), lambda i: (0, 0)),
            pl.BlockSpec((PEER_HEADS, N_KEYS, D_HALF), lambda i: (0, 0, 0)),
            pl.BlockSpec((PEER_HEADS, N_KEYS, D_HALF), lambda i: (0, 0, 0)),
        ],
        out_specs=[pl.BlockSpec((d, tb), tok)] + [pl.BlockSpec((rows, tb), tok)] * 4,
        out_shape=[jax.ShapeDtypeStruct((d, n), BF16)] + [jax.ShapeDtypeStruct((rows, n), F32)] * 4,
        scratch_shapes=[
            pltpu.VMEM((tb, dq), BF16),
            pltpu.VMEM((2, rows, tb), F32),
            pltpu.VMEM((2, TOPK * PEER_HEADS, tb), F32),
        ],
        compiler_params=_params("parallel"),
        name="peer_route",
    )(xn, wq, k1, k2)


def _peer_kernel(xt_ref, s2_ref, e2_ref, th_ref, e1_ref, u_ref, vt_ref, o_ref, h0, h1, a0, a1, *, nblk):
    eb, tb = h0.shape
    d = o_ref.shape[0]
    n_i1 = eb // N_KEYS
    rb = 2 * SUBLANES
    n_lb = tb // LANES
    hrows = eb // n_lb
    orows = d // n_lb
    s = pl.program_id(0)

    @pl.when(s == 0)
    def _():
        for ref in (h0, h1, a0, a1):
            ref[...] = jnp.zeros_like(ref)

    @pl.when((s < 2) | ((s - 2) % nblk == 0))
    def _():
        o_ref[...] = jnp.zeros_like(o_ref)

    def step_part(h_new, h_prev, a_prev, a_old, lb):
        hr = pl.ds(lb * hrows, hrows)
        h_new[hr, :] = jnp.dot(u_ref[hr, :], xt_ref[...], preferred_element_type=F32)

        cols = pl.ds(lb * LANES, LANES)
        for r0 in range(0, N_KEYS, rb):
            gate = [None] * n_i1
            for h in range(PEER_HEADS):
                s2 = s2_ref[h, r0:r0 + rb, cols]
                e2 = e2_ref[h, r0:r0 + rb, cols]
                for a in range(n_i1):
                    t = jnp.where(s2 >= th_ref[h, a:a + 1, cols], e2, 0.0) * e1_ref[h, a:a + 1, cols]
                    gate[a] = t if gate[a] is None else gate[a] + t
            for a in range(n_i1):
                hid = h_prev[a * N_KEYS + r0:a * N_KEYS + r0 + rb, cols]
                act = 0.5 * hid * (1.0 + lax.erf(hid * (1.0 / math.sqrt(2.0))))
                a_prev[a * N_KEYS + r0:a * N_KEYS + r0 + rb, cols] = (act * gate[a]).astype(BF16)

        orow = pl.ds(lb * orows, orows)
        o_ref[orow, :] += jnp.dot(vt_ref[orow, :], a_old[...], preferred_element_type=F32)

    @pl.when(s % 2 == 0)
    def _():
        for lb in range(n_lb):
            step_part(h0, h1, a1, a0, lb)

    @pl.when(s % 2 == 1)
    def _():
        for lb in range(n_lb):
            step_part(h1, h0, a0, a1, lb)


def _peer(xt, s2, e2, th, e1, u, vt):
    d, n = xt.shape
    n_exp = u.shape[0]
    tb = _tile(n, 512)
    eb = 1024
    n_i1 = eb // N_KEYS
    nblk = n_exp // eb
    n_tiles = n // tb
    last = n_tiles * nblk - 1
    r3 = lambda a: a.reshape(PEER_HEADS, N_KEYS, n)
    step = lambda s, lag: jnp.clip(s - lag, 0, last)
    tile = lambda s, lag: step(s, lag) // nblk
    blk = lambda s, lag: step(s, lag) % nblk
    return pl.pallas_call(
        functools.partial(_peer_kernel, nblk=nblk),
        grid=(n_tiles * nblk + 2,),
        in_specs=[
            pl.BlockSpec((d, tb), lambda s: (0, tile(s, 0))),
            pl.BlockSpec((PEER_HEADS, N_KEYS, tb), lambda s: (0, 0, tile(s, 1))),
            pl.BlockSpec((PEER_HEADS, N_KEYS, tb), lambda s: (0, 0, tile(s, 1))),
            pl.BlockSpec((PEER_HEADS, n_i1, tb), lambda s: (0, blk(s, 1), tile(s, 1))),
            pl.BlockSpec((PEER_HEADS, n_i1, tb), lambda s: (0, blk(s, 1), tile(s, 1))),
            pl.BlockSpec((eb, d), lambda s: (blk(s, 0), 0)),
            pl.BlockSpec((d, eb), lambda s: (0, blk(s, 2))),
        ],
        out_specs=pl.BlockSpec((d, tb), lambda s: (0, tile(s, 2))),
        out_shape=jax.ShapeDtypeStruct((d, n), F32),
        scratch_shapes=[pltpu.VMEM((eb, tb), F32)] * 2 + [pltpu.VMEM((eb, tb), BF16)] * 2,
        compiler_params=_params("arbitrary"),
        name="peer_dense",
    )(xt, r3(s2), r3(e2), r3(th), r3(e1), u, vt)


def _final_kernel(h_ref, pt_ref, g_ref, y_ref, *, normalize):
    y = h_ref[...] + pt_ref[...].T
    if normalize:
        r = lax.rsqrt(jnp.mean(y * y, axis=-1, keepdims=True) + EPS)
        y = y * r * g_ref[...]
    y_ref[...] = y


def _final(h, pt, g, normalize):
    n, d = h.shape
    tm = _tile(n, 512)
    return pl.pallas_call(
        functools.partial(_final_kernel, normalize=normalize),
        grid=(n // tm,),
        in_specs=[
            pl.BlockSpec((tm, d), lambda i: (i, 0)),
            pl.BlockSpec((d, tm), lambda i: (0, i)),
            pl.BlockSpec((1, d), lambda i: (0, 0)),
        ],
        out_specs=pl.BlockSpec((tm, d), lambda i: (i, 0)),
        out_shape=jax.ShapeDtypeStruct((n, d), F32),
        compiler_params=_params("parallel"),
        name="final",
    )(h, pt, g.reshape(1, d))


def _layer(x, seg_len, prev_k, prev_v, prev_conv, w, norm_final_g, last):
    n, d = x.shape
    n_seg = n // seg_len
    d_q = N_HEADS * HEAD_DIM
    d_kv = N_KV_HEADS * HEAD_DIM
    d_conv = w["conv_w"].shape[1]
    proj = _inproj(x, w["norm_mix_g"], w["w_in"])
    k_col, v_col = d_q // d_kv, d_q // d_kv + 1
    o_b = d_q + 2 * d_kv
    b_col, c_col, x_col = o_b // 512, (o_b + d_conv) // 512, (o_b + 2 * d_conv) // 512
    ga_col, gb_col = (o_b + 3 * d_conv) // 512, (o_b + 3 * d_conv + d) // 512
    assert o_b % 512 == 0 and d_conv % 512 == 0 and d % 512 == 0 and d_q % d_kv == 0

    if prev_k is None:
        tt = _tile(seg_len, 512)
        assert tt % WINDOW == 0
        tps = seg_len // tt
        halo = lambda col: (lambda i: (jnp.maximum(i * (tt // WINDOW) - 1, 0), col))
        attn = _attention(w["attn_sinks"], proj, 0, proj, k_col, proj, v_col,
                          proj, halo(k_col), proj, halo(v_col), n, tt, tps)
        tc = tt
        chalo = lambda col: (lambda i, j: (jnp.maximum(i * (tc // SUBLANES) - 1, 0), col + j))
        bc, utail = _conv(proj, b_col, c_col, x_col, proj, chalo(c_col), proj, chalo(x_col),
                          w["conv_w"], w["conv_b"], tc, seg_len // tc)
        new_k = proj.reshape(n_seg, seg_len, -1)[:, -WINDOW:, d_q:d_q + d_kv]
        new_v = proj.reshape(n_seg, seg_len, -1)[:, -WINDOW:, d_q + d_kv:d_q + 2 * d_kv]
    else:
        assert seg_len == CHUNK
        pk = prev_k.reshape(n_seg * WINDOW, d_kv)
        pv = prev_v.reshape(n_seg * WINDOW, d_kv)
        seg = lambda i: (i, 0)
        attn = _attention(w["attn_sinks"], proj, 0, proj, k_col, proj, v_col,
                          pk, seg, pv, seg, n, seg_len, 0)
        hist = jnp.pad(prev_conv, ((0, 0), (SUBLANES - (CONV_W - 1), 0), (0, 0))).reshape(n_seg * SUBLANES, d_conv)
        hmap = lambda i, j: (i, j)
        bc, utail = _conv(proj, b_col, c_col, x_col, hist, hmap, jnp.ones_like(hist), hmap,
                          w["conv_w"], w["conv_b"], seg_len, 0)
        k_new = proj[:, d_q:d_q + d_kv].reshape(n_seg, seg_len, d_kv)
        v_new = proj[:, d_q + d_kv:d_q + 2 * d_kv].reshape(n_seg, seg_len, d_kv)
        new_k = jnp.concatenate([prev_k.reshape(n_seg, WINDOW, d_kv), k_new], axis=1)[:, -WINDOW:]
        new_v = jnp.concatenate([prev_v.reshape(n_seg, WINDOW, d_kv), v_new], axis=1)[:, -WINDOW:]
    new_conv = utail.reshape(n_seg, -1, SUBLANES, d_conv)[:, -1, SUBLANES - (CONV_W - 1):]

    merged = _merge(attn, bc, w["w_proj_a"], w["w_proj_b"], proj, ga_col, gb_col)
    h, xn = _outproj(merged, x, w["w_out"], w["norm_ffn_g"])
    xt, s2, e2, th, e1 = _route(xn, w["peer_w_query"], w["peer_keys1"], w["peer_keys2"])
    pt = _peer(xt, s2, e2, th, e1, w["peer_u"], w["peer_vt"])
    y = _final(h, pt, norm_final_g, last)
    shape5 = (n_seg, WINDOW, N_KV_HEADS, HEAD_DIM)
    return y, new_k.reshape(shape5), new_v.reshape(shape5), new_conv


def kernel(x_prompt, x_sample, state_attn_k, state_attn_v, state_conv, norm_mix_g, w_in, attn_sinks, conv_w,
           conv_b, w_proj_a, w_proj_b, w_out, norm_ffn_g, peer_w_query, peer_keys1, peer_keys2, peer_u, peer_v,
           norm_final_g):
    depth = w_in.shape[0]
    bp, sp, d = x_prompt.shape
    bs, ss, _ = x_sample.shape
    yp = x_prompt.reshape(bp * sp, d)
    ys = x_sample.reshape(bs * ss, d)
    outs = [[] for _ in range(6)]
    for l in range(depth):
        w = dict(
            norm_mix_g=norm_mix_g[l], w_in=w_in[l].astype(BF16), attn_sinks=attn_sinks[l],
            conv_w=conv_w[l], conv_b=conv_b[l], w_proj_a=w_proj_a[l].astype(BF16),
            w_proj_b=w_proj_b[l].astype(BF16), w_out=w_out[l].astype(BF16), norm_ffn_g=norm_ffn_g[l],
            peer_w_query=peer_w_query[l].astype(BF16), peer_keys1=peer_keys1[l].astype(BF16),
            peer_keys2=peer_keys2[l].astype(BF16), peer_u=peer_u[l].astype(BF16),
            peer_vt=peer_v[l].astype(BF16).T,
        )
        last = l == depth - 1
        yp, k1, v1, c1 = _layer(yp, sp, None, None, None, w, norm_final_g, last)
        ys, k2, v2, c2 = _layer(ys, ss, state_attn_k[l], state_attn_v[l], state_conv[l], w, norm_final_g, last)
        for lst, val in zip(outs, (k1, v1, c1, k2, v2, c2)):
            lst.append(val)
    return (yp.reshape(bp, sp, d), ys.reshape(bs, ss, d)) + tuple(jnp.stack(o) for o in outs)
```

```python
import functools
import math

import jax
import jax.numpy as jnp
from jax import lax
from jax.experimental import pallas as pl
from jax.experimental.pallas import tpu as pltpu

F32 = jnp.float32
BF16 = jnp.bfloat16

CHUNK = 64
N_HEADS = 32
N_KV_HEADS = 4
GROUP = N_HEADS // N_KV_HEADS
HEAD_DIM = 64
WINDOW = 128
BAND = WINDOW + CHUNK
CONV_W = 3
PEER_HEADS = 8
N_KEYS = 128
D_HALF = 128
TOPK = 16
EPS = 1e-6
NEG_INF = -1e30

SUBLANES = 8
LANES = 128
VMEM_LIMIT = 56 * 1024 * 1024

_NT = (((1,), (1,)), ((), ()))


def _params(*sem, flags=None):
    return pltpu.CompilerParams(dimension_semantics=sem, vmem_limit_bytes=VMEM_LIMIT, flags=flags)


def _tile(n, pref):
    t = min(n, pref)
    while n % t:
        t //= 2
    return t


def _oddeven_merge_sort(n):
    pairs = []

    def merge(lo, m, r):
        step = r * 2
        if step < m:
            merge(lo, m, step)
            merge(lo + r, m, step)
            for i in range(lo + r, lo + m - r, step):
                pairs.append((i, i + r))
        else:
            pairs.append((lo, lo + r))

    def sort(lo, m):
        if m > 1:
            half = m // 2
            sort(lo, half)
            sort(lo + half, half)
            merge(lo, m, 1)

    sort(0, n)
    return pairs


def _bitonic_merge(n):
    pairs = []
    d = n // 2
    while d >= 1:
        for i in range(n):
            if (i & d) == 0:
                pairs.append((i, i + d))
        d //= 2
    return pairs


_SORT16 = _oddeven_merge_sort(TOPK)
_BITONIC16 = _bitonic_merge(TOPK)


def _apply_net(pairs, xs):
    xs = list(xs)
    for i, j in pairs:
        a, b = xs[i], xs[j]
        xs[i] = jnp.maximum(a, b)
        xs[j] = jnp.minimum(a, b)
    return xs


def _top16_bitonic(a, b):
    return [jnp.maximum(a[i], b[TOPK - 1 - i]) for i in range(TOPK)]


def _inproj_kernel(x_ref, g_ref, w_ref, o_ref, xn_ref):
    @pl.when(pl.program_id(1) == 0)
    def _():
        x = x_ref[...]
        r = lax.rsqrt(jnp.mean(x * x, axis=-1, keepdims=True) + EPS)
        xn_ref[...] = (x * r * g_ref[...]).astype(BF16)

    o_ref[...] = jnp.dot(xn_ref[...], w_ref[...], preferred_element_type=F32)


def _inproj(x, g, w):
    n, d = x.shape
    d_in = w.shape[1]
    tm = _tile(n, 1024)
    tn = 512
    return pl.pallas_call(
        _inproj_kernel,
        grid=(n // tm, d_in // tn),
        in_specs=[
            pl.BlockSpec((tm, d), lambda i, j: (i, 0)),
            pl.BlockSpec((1, d), lambda i, j: (0, 0)),
            pl.BlockSpec((d, tn), lambda i, j: (0, j)),
        ],
        out_specs=pl.BlockSpec((tm, tn), lambda i, j: (i, j)),
        out_shape=jax.ShapeDtypeStruct((n, d_in), F32),
        scratch_shapes=[pltpu.VMEM((tm, d), BF16)],
        compiler_params=_params("parallel", "arbitrary"),
        name="inproj",
    )(x, g.reshape(1, d), w)


def _attn_kernel(sink_ref, q_ref, k_ref, v_ref, pk_ref, pv_ref, o_ref, kall, vall, *,
                 n_chunks, tiles_per_seg):
    kall[0:WINDOW, :] = pk_ref[...].astype(BF16)
    vall[0:WINDOW, :] = pv_ref[...].astype(BF16)
    kall[WINDOW:, :] = k_ref[...].astype(BF16)
    vall[WINDOW:, :] = v_ref[...].astype(BF16)
    nq = GROUP * CHUNK
    if tiles_per_seg:
        band_chunk = lax.broadcasted_iota(jnp.int32, (BAND, nq), 0) // CHUNK
    q_group = lax.broadcasted_iota(jnp.int32, (1, nq), 1) // CHUNK
    sink_rows = []
    for kh in range(N_KV_HEADS):
        row = jnp.zeros((1, nq), F32)
        for g in range(GROUP):
            row = jnp.where(q_group == g, sink_ref[kh * GROUP + g], row)
        sink_rows.append(row)

    def chunk_pair(masked, i, carry):
        for u in range(unroll):
            chunk_body(masked, i * unroll + u)
        return carry

    def chunk_body(masked, c):
        r0 = pl.multiple_of(c * CHUNK, CHUNK)
        qc = q_ref[pl.ds(r0, CHUNK), :] * (HEAD_DIM ** -0.5)
        kb = kall[pl.ds(r0, BAND), :]
        vb = vall[pl.ds(r0, BAND), :]
        if masked:
            valid = (c + band_chunk) >= (WINDOW // CHUNK)
        outs = []
        for kh in range(N_KV_HEADS):
            k_h = kb[:, kh * HEAD_DIM:(kh + 1) * HEAD_DIM]
            v_h = vb[:, kh * HEAD_DIM:(kh + 1) * HEAD_DIM]
            q_h = jnp.concatenate(
                [qc[:, (kh * GROUP + g) * HEAD_DIM:(kh * GROUP + g + 1) * HEAD_DIM] for g in range(GROUP)],
                axis=0).astype(BF16)
            s = lax.dot_general(k_h, q_h, _NT, preferred_element_type=F32)
            if masked:
                s = jnp.where(valid, s, NEG_INF)
            sink = sink_rows[kh]
            m = jnp.maximum(jnp.max(s, axis=0, keepdims=True), sink)
            p = jnp.exp(s - m)
            probs = p / (jnp.sum(p, axis=0, keepdims=True) + jnp.exp(sink - m))
            o_t = lax.dot_general(v_h, probs.astype(BF16), (((0,), (0,)), ((), ())),
                                  preferred_element_type=F32)
            o = o_t.T
            outs += [o[g * CHUNK:(g + 1) * CHUNK, :] for g in range(GROUP)]
        o_ref[pl.ds(r0, CHUNK), :] = jnp.concatenate(outs, axis=1).astype(BF16)

    unroll = 2 if n_chunks % 2 == 0 else 1
    if tiles_per_seg:
        first = pl.program_id(0) % tiles_per_seg == 0

        @pl.when(first)
        def _():
            lax.fori_loop(0, n_chunks // unroll, functools.partial(chunk_pair, True), 0)

        @pl.when(jnp.logical_not(first))
        def _():
            lax.fori_loop(0, n_chunks // unroll, functools.partial(chunk_pair, False), 0)
    else:
        lax.fori_loop(0, n_chunks // unroll, functools.partial(chunk_pair, False), 0)


def _attention(sinks, q_arr, q_col, k_arr, k_col, v_arr, v_col, pk_arr, pk_map, pv_arr, pv_map,
               n, tt, tiles_per_seg):
    d_q = N_HEADS * HEAD_DIM
    d_kv = N_KV_HEADS * HEAD_DIM
    body = functools.partial(_attn_kernel, n_chunks=tt // CHUNK, tiles_per_seg=tiles_per_seg)
    return pl.pallas_call(
        body,
        grid=(n // tt,),
        in_specs=[
            pl.BlockSpec(memory_space=pltpu.SMEM),
            pl.BlockSpec((tt, d_q), lambda i: (i, q_col)),
            pl.BlockSpec((tt, d_kv), lambda i: (i, k_col)),
            pl.BlockSpec((tt, d_kv), lambda i: (i, v_col)),
            pl.BlockSpec((WINDOW, d_kv), pk_map),
            pl.BlockSpec((WINDOW, d_kv), pv_map),
        ],
        out_specs=pl.BlockSpec((tt, d_q), lambda i: (i, 0)),
        out_shape=jax.ShapeDtypeStruct((n, d_q), BF16),
        scratch_shapes=[pltpu.VMEM((WINDOW + tt, d_kv), BF16), pltpu.VMEM((WINDOW + tt, d_kv), BF16)],
        compiler_params=_params("arbitrary"),
        name="attention",
    )(sinks, q_arr, k_arr, v_arr, pk_arr, pv_arr)


def _conv_kernel(b_ref, c_ref, x_ref, hc_ref, hx_ref, w_ref, cb_ref, bc_ref, ut_ref, *, tiles_per_seg):
    u = c_ref[...] * x_ref[...]
    uh = hc_ref[...] * hx_ref[...]
    if tiles_per_seg:
        uh = jnp.where(pl.program_id(0) % tiles_per_seg == 0, 0.0, uh)
    row = lax.broadcasted_iota(jnp.int32, u.shape, 0)
    um1 = jnp.where(row == 0, uh[7:8, :], pltpu.roll(u, 1, axis=0))
    um2 = jnp.where(row == 0, uh[6:7, :], jnp.where(row == 1, uh[7:8, :], pltpu.roll(u, 2, axis=0)))
    conv = cb_ref[...] + w_ref[0:1, :] * um2
    conv = conv + w_ref[1:2, :] * um1
    conv = conv + w_ref[2:3, :] * u
    bc_ref[...] = (b_ref[...] * conv).astype(BF16)
    ut_ref[...] = u[u.shape[0] - SUBLANES:, :]


def _conv(proj, b_col, c_col, x_col, hc_arr, hc_map, hx_arr, hx_map, conv_w, conv_b, tm, tiles_per_seg):
    n = proj.shape[0]
    d_conv = conv_w.shape[1]
    tn = 512
    nh = d_conv // tn
    body = functools.partial(_conv_kernel, tiles_per_seg=tiles_per_seg)
    return pl.pallas_call(
        body,
        grid=(n // tm, nh),
        in_specs=[
            pl.BlockSpec((tm, tn), lambda i, j: (i, b_col + j)),
            pl.BlockSpec((tm, tn), lambda i, j: (i, c_col + j)),
            pl.BlockSpec((tm, tn), lambda i, j: (i, x_col + j)),
            pl.BlockSpec((SUBLANES, tn), hc_map),
            pl.BlockSpec((SUBLANES, tn), hx_map),
            pl.BlockSpec((CONV_W, tn), lambda i, j: (0, j)),
            pl.BlockSpec((1, tn), lambda i, j: (0, j)),
        ],
        out_specs=[
            pl.BlockSpec((tm, tn), lambda i, j: (i, j)),
            pl.BlockSpec((SUBLANES, tn), lambda i, j: (i, j)),
        ],
        out_shape=[
            jax.ShapeDtypeStruct((n, d_conv), BF16),
            jax.ShapeDtypeStruct((n // tm * SUBLANES, d_conv), F32),
        ],
        compiler_params=_params("arbitrary", "arbitrary"),
        name="conv",
    )(proj, proj, proj, hc_arr, hx_arr, conv_w, conv_b.reshape(1, d_conv))


def _merge_kernel(a_ref, bc_ref, wa_ref, wb_ref, ga_ref, gb_ref, o_ref):
    ya = jnp.dot(a_ref[...], wa_ref[...], preferred_element_type=F32)
    yb = jnp.dot(bc_ref[...], wb_ref[...], preferred_element_type=F32)
    o_ref[...] = (jax.nn.sigmoid(ga_ref[...]) * ya + jax.nn.sigmoid(gb_ref[...]) * yb).astype(BF16)


def _merge(attn, bc, wa, wb, proj, ga_col, gb_col):
    n, d_q = attn.shape
    d_conv = bc.shape[1]
    d = wa.shape[1]
    tm = _tile(n, 1024)
    tn = 512
    return pl.pallas_call(
        _merge_kernel,
        grid=(n // tm, d // tn),
        in_specs=[
            pl.BlockSpec((tm, d_q), lambda i, j: (i, 0)),
            pl.BlockSpec((tm, d_conv), lambda i, j: (i, 0)),
            pl.BlockSpec((d_q, tn), lambda i, j: (0, j)),
            pl.BlockSpec((d_conv, tn), lambda i, j: (0, j)),
            pl.BlockSpec((tm, tn), lambda i, j: (i, ga_col + j)),
            pl.BlockSpec((tm, tn), lambda i, j: (i, gb_col + j)),
        ],
        out_specs=pl.BlockSpec((tm, tn), lambda i, j: (i, j)),
        out_shape=jax.ShapeDtypeStruct((n, d), BF16),
        compiler_params=_params("parallel", "arbitrary"),
        name="merge",
    )(attn, bc, wa, wb, proj, proj)


def _outproj_kernel(m_ref, x_ref, w_ref, g_ref, h_ref, xn_ref):
    h = x_ref[...] + jnp.dot(m_ref[...], w_ref[...], preferred_element_type=F32)
    h_ref[...] = h
    r = lax.rsqrt(jnp.mean(h * h, axis=-1, keepdims=True) + EPS)
    xn_ref[...] = (h * r * g_ref[...]).astype(BF16)


def _outproj(merged, x, w, g):
    n, d = x.shape
    tm = _tile(n, 512)
    return pl.pallas_call(
        _outproj_kernel,
        grid=(n // tm,),
        in_specs=[
            pl.BlockSpec((tm, d), lambda i: (i, 0)),
            pl.BlockSpec((tm, d), lambda i: (i, 0)),
            pl.BlockSpec((d, d), lambda i: (0, 0)),
            pl.BlockSpec((1, d), lambda i: (0, 0)),
        ],
        out_specs=[pl.BlockSpec((tm, d), lambda i: (i, 0)), pl.BlockSpec((tm, d), lambda i: (i, 0))],
        out_shape=[jax.ShapeDtypeStruct((n, d), F32), jax.ShapeDtypeStruct((n, d), BF16)],
        compiler_params=_params("parallel"),
        name="outproj",
    )(merged, x, w, g.reshape(1, d))


def _pair_threshold(v1, v2):
    c = lambda i, j: v1[i] + v2[j]
    row0 = [c(0, j) for j in range(TOPK)]
    grp1 = [c(1, j) for j in range(8)] + [c(2, j) for j in range(5)] + [c(3, j) for j in range(3)]
    grp2 = ([c(3, 3)] + [c(4, j) for j in range(3)] + [c(5, 0), c(5, 1), c(6, 0), c(6, 1), c(7, 0), c(7, 1)]
            + [c(i, 0) for i in range(8, 14)])
    top = _apply_net(_BITONIC16, _top16_bitonic(row0, _apply_net(_SORT16, grp1)))
    top = _apply_net(_BITONIC16, _top16_bitonic(top, _apply_net(_SORT16, grp2)))
    top[TOPK - 1] = jnp.maximum(top[TOPK - 1], c(14, 0))
    top[TOPK - 2] = jnp.maximum(top[TOPK - 2], c(15, 0))
    tau = functools.reduce(jnp.minimum, top)
    m = row0[0]
    z = functools.reduce(lambda a, b: a + b, [jnp.exp(t - m) for t in top])
    return tau, z


TOP_R = 3
GEN_J = TOPK // (TOP_R + 1)


def _route_kernel(xn_ref, wq_ref, k1_ref, k2_ref, xt_ref, s2_ref, e2_ref, th_ref, e1_ref,
                  q_scr, s_scr, top_scr):
    tb = xn_ref.shape[0]
    q_scr[...] = jnp.dot(xn_ref[...], wq_ref[...], preferred_element_type=F32).astype(BF16)
    xt_ref[...] = xn_ref[...].astype(F32).T.astype(BF16)
    for h in range(PEER_HEADS):
        rows = slice(h * N_KEYS, (h + 1) * N_KEYS)
        q1 = q_scr[:, 2 * h * D_HALF:(2 * h + 1) * D_HALF]
        q2 = q_scr[:, (2 * h + 1) * D_HALF:(2 * h + 2) * D_HALF]
        s_scr[0, rows, :] = lax.dot_general(k1_ref[h], q1, _NT, preferred_element_type=F32)
        s_scr[1, rows, :] = lax.dot_general(k2_ref[h], q2, _NT, preferred_element_type=F32)

    def lane_block(lb, carry):
        cols = pl.ds(pl.multiple_of(lb * LANES, LANES), LANES)
        for side in range(2):
            for h in range(PEER_HEADS):
                lst = [s_scr[side, pl.ds(h * N_KEYS + SUBLANES * r, SUBLANES), cols]
                       for r in range(N_KEYS // SUBLANES)]
                lst = _apply_net(_SORT16, lst)
                for d in (4, 2, 1):
                    other = [pltpu.roll(x, d, axis=0) for x in lst]
                    lst = _apply_net(_BITONIC16, _top16_bitonic(lst, other))
                for i in range(TOPK):
                    top_scr[side, pl.ds(i * PEER_HEADS + h, 1), cols] = lst[i][0:1, :]
        v1 = [top_scr[0, pl.ds(i * PEER_HEADS, PEER_HEADS), cols] for i in range(TOPK)]
        v2 = [top_scr[1, pl.ds(i * PEER_HEADS, PEER_HEADS), cols] for i in range(TOPK)]
        tau, z = _pair_threshold(v1, v2)
        reach = []
        for j in range(GEN_J):
            t = jnp.full_like(tau, jnp.inf)
            for r in range(TOPK // (j + 1)):
                t = jnp.where(v1[r] + v2[j] >= tau, v1[r], t)
            reach.append(t)
        floor_top = []
        for r in range(TOP_R):
            t = jnp.full_like(tau, jnp.inf)
            for j in range(TOPK // (r + 1)):
                t = jnp.where(v1[r] + v2[j] >= tau, v2[j], t)
            floor_top.append(t)
        for h in range(PEER_HEADS):
            rows = slice(h * N_KEYS, (h + 1) * N_KEYS)
            s1 = s_scr[0, rows, cols]
            s2 = s_scr[1, rows, cols]
            th = jnp.full_like(s1, jnp.inf)
            for j in range(GEN_J):
                th = jnp.where(s1 >= reach[j][h:h + 1, :], v2[j][h:h + 1, :], th)
            for r in reversed(range(TOP_R)):
                th = jnp.where(s1 >= v1[r][h:h + 1, :], floor_top[r][h:h + 1, :], th)
            th_ref[rows, cols] = th
            e1_ref[rows, cols] = jnp.exp(s1 - v1[0][h:h + 1, :])
            s2_ref[rows, cols] = s2
            e2_ref[rows, cols] = jnp.exp(s2 - v2[0][h:h + 1, :]) / z[h:h + 1, :]
        return carry

    lax.fori_loop(0, tb // LANES, lane_block, 0)


def _route(xn, wq, k1, k2):
    n, d = xn.shape
    dq = wq.shape[1]
    tb = _tile(n, 512)
    rows = PEER_HEADS * N_KEYS
    tok = lambda i: (0, i)
    return pl.pallas_call(
        _route_kernel,
        grid=(n // tb,),
        in_specs=[
            pl.BlockSpec((tb, d), lambda i: (i, 0)),
            pl.BlockSpec((d, dq), lambda i: (0, 0)),
            pl.BlockSpec((PEER_HEADS, N_KEYS, D_HALF), lambda i: (0, 0, 0)),
            pl.BlockSpec((PEER_HEADS, N_KEYS, D_HALF), lambda i: (0, 0, 0)),
        ],
        out_specs=[pl.BlockSpec((d, tb), tok)] + [pl.BlockSpec((rows, tb), tok)] * 4,
        out_shape=[jax.ShapeDtypeStruct((d, n), BF16)] + [jax.ShapeDtypeStruct((rows, n), F32)] * 4,
        scratch_shapes=[
            pltpu.VMEM((tb, dq), BF16),
            pltpu.VMEM((2, rows, tb), F32),
            pltpu.VMEM((2, TOPK * PEER_HEADS, tb), F32),
        ],
        compiler_params=_params("parallel"),
        name="peer_route",
    )(xn, wq, k1, k2)


def _peer_kernel(xt_ref, s2_ref, e2_ref, th_ref, e1_ref, u_ref, vt_ref, res_ref, g_ref, y_ref,
                 o_ref, h0, h1, a0, a1, *, nblk, normalize):
    eb, tb = h0.shape
    d = o_ref.shape[0]
    n_i1 = eb // N_KEYS
    rb = 2 * SUBLANES
    n_lb = tb // LANES
    hrows = eb // n_lb
    orows = d // n_lb
    s = pl.program_id(0)

    @pl.when(s == 0)
    def _():
        for ref in (h0, h1, a0, a1):
            ref[...] = jnp.zeros_like(ref)

    @pl.when((s < 2) | ((s - 2) % nblk == 0))
    def _():
        o_ref[...] = jnp.zeros_like(o_ref)

    def step_part(h_new, h_prev, a_prev, a_old, lb):
        hr = pl.ds(lb * hrows, hrows)
        h_new[hr, :] = jnp.dot(u_ref[hr, :], xt_ref[...], preferred_element_type=F32)

        cols = pl.ds(lb * LANES, LANES)
        for r0 in range(0, N_KEYS, rb):
            gate = [None] * n_i1
            for h in range(PEER_HEADS):
                s2 = s2_ref[h, r0:r0 + rb, cols]
                e2 = e2_ref[h, r0:r0 + rb, cols]
                for a in range(n_i1):
                    t = jnp.where(s2 >= th_ref[h, a:a + 1, cols], e2, 0.0) * e1_ref[h, a:a + 1, cols]
                    gate[a] = t if gate[a] is None else gate[a] + t
            for a in range(n_i1):
                hid = h_prev[a * N_KEYS + r0:a * N_KEYS + r0 + rb, cols]
                act = 0.5 * hid * (1.0 + lax.erf(hid * (1.0 / math.sqrt(2.0))))
                a_prev[a * N_KEYS + r0:a * N_KEYS + r0 + rb, cols] = (act * gate[a]).astype(BF16)

        orow = pl.ds(lb * orows, orows)
        o_ref[orow, :] += jnp.dot(vt_ref[orow, :], a_old[...], preferred_element_type=F32)

    @pl.when(s % 2 == 0)
    def _():
        for lb in range(n_lb):
            step_part(h0, h1, a1, a0, lb)

    @pl.when(s % 2 == 1)
    def _():
        for lb in range(n_lb):
            step_part(h1, h0, a0, a1, lb)

    @pl.when((s >= 2) & ((s - 2) % nblk == nblk - 1))
    def _():
        y = res_ref[...] + o_ref[...].T
        if normalize:
            r = lax.rsqrt(jnp.mean(y * y, axis=-1, keepdims=True) + EPS)
            y = y * r * g_ref[...]
        y_ref[...] = y


def _peer(xt, s2, e2, th, e1, u, vt, res, g, normalize):
    d, n = xt.shape
    n_exp = u.shape[0]
    tb = _tile(n, 512)
    eb = 1024
    n_i1 = eb // N_KEYS
    nblk = n_exp // eb
    n_tiles = n // tb
    last = n_tiles * nblk - 1
    r3 = lambda a: a.reshape(PEER_HEADS, N_KEYS, n)
    step = lambda s, lag: jnp.clip(s - lag, 0, last)
    tile = lambda s, lag: step(s, lag) // nblk
    blk = lambda s, lag: step(s, lag) % nblk
    return pl.pallas_call(
        functools.partial(_peer_kernel, nblk=nblk, normalize=normalize),
        grid=(n_tiles * nblk + 2,),
        in_specs=[
            pl.BlockSpec((d, tb), lambda s: (0, tile(s, 0))),
            pl.BlockSpec((PEER_HEADS, N_KEYS, tb), lambda s: (0, 0, tile(s, 1))),
            pl.BlockSpec((PEER_HEADS, N_KEYS, tb), lambda s: (0, 0, tile(s, 1))),
            pl.BlockSpec((PEER_HEADS, n_i1, tb), lambda s: (0, blk(s, 1), tile(s, 1))),
            pl.BlockSpec((PEER_HEADS, n_i1, tb), lambda s: (0, blk(s, 1), tile(s, 1))),
            pl.BlockSpec((eb, d), lambda s: (blk(s, 0), 0)),
            pl.BlockSpec((d, eb), lambda s: (0, blk(s, 2))),
            pl.BlockSpec((tb, d), lambda s: (tile(s, 2), 0), pipeline_mode=pl.Buffered(1)),
            pl.BlockSpec((1, d), lambda s: (0, 0)),
        ],
        out_specs=pl.BlockSpec((tb, d), lambda s: (tile(s, 2), 0)),
        out_shape=jax.ShapeDtypeStruct((n, d), F32),
        scratch_shapes=([pltpu.VMEM((d, tb), F32)] + [pltpu.VMEM((eb, tb), F32)] * 2
                        + [pltpu.VMEM((eb, tb), BF16)] * 2),
        compiler_params=_params("arbitrary"),
        name="peer_dense",
    )(xt, r3(s2), r3(e2), r3(th), r3(e1), u, vt, res, g.reshape(1, d))


def _transpose_cast_kernel(x_ref, o_ref):
    o_ref[...] = x_ref[...].T.astype(o_ref.dtype)


def _transpose_cast(x, dtype):
    r, c = x.shape
    tr = _tile(r, 512)
    return pl.pallas_call(
        _transpose_cast_kernel,
        grid=(r // tr,),
        in_specs=[pl.BlockSpec((tr, c), lambda i: (i, 0))],
        out_specs=pl.BlockSpec((c, tr), lambda i: (0, i)),
        out_shape=jax.ShapeDtypeStruct((c, r), dtype),
        compiler_params=_params("parallel"),
        name="transpose_cast",
    )(x)


def _layer(x, seg_len, prev_k, prev_v, prev_conv, w, norm_final_g, last):
    n, d = x.shape
    n_seg = n // seg_len
    d_q = N_HEADS * HEAD_DIM
    d_kv = N_KV_HEADS * HEAD_DIM
    d_conv = w["conv_w"].shape[1]
    proj = _inproj(x, w["norm_mix_g"], w["w_in"])
    k_col, v_col = d_q // d_kv, d_q // d_kv + 1
    o_b = d_q + 2 * d_kv
    b_col, c_col, x_col = o_b // 512, (o_b + d_conv) // 512, (o_b + 2 * d_conv) // 512
    ga_col, gb_col = (o_b + 3 * d_conv) // 512, (o_b + 3 * d_conv + d) // 512
    assert o_b % 512 == 0 and d_conv % 512 == 0 and d % 512 == 0 and d_q % d_kv == 0

    if prev_k is None:
        tt = _tile(seg_len, 512)
        assert tt % WINDOW == 0
        tps = seg_len // tt
        halo = lambda col: (lambda i: (jnp.maximum(i * (tt // WINDOW) - 1, 0), col))
        attn = _attention(w["attn_sinks"], proj, 0, proj, k_col, proj, v_col,
                          proj, halo(k_col), proj, halo(v_col), n, tt, tps)
        tc = tt
        chalo = lambda col: (lambda i, j: (jnp.maximum(i * (tc // SUBLANES) - 1, 0), col + j))
        bc, utail = _conv(proj, b_col, c_col, x_col, proj, chalo(c_col), proj, chalo(x_col),
                          w["conv_w"], w["conv_b"], tc, seg_len // tc)
        new_k = proj.reshape(n_seg, seg_len, -1)[:, -WINDOW:, d_q:d_q + d_kv]
        new_v = proj.reshape(n_seg, seg_len, -1)[:, -WINDOW:, d_q + d_kv:d_q + 2 * d_kv]
    else:
        assert seg_len == CHUNK
        pk = prev_k.reshape(n_seg * WINDOW, d_kv)
        pv = prev_v.reshape(n_seg * WINDOW, d_kv)
        seg = lambda i: (i, 0)
        attn = _attention(w["attn_sinks"], proj, 0, proj, k_col, proj, v_col,
                          pk, seg, pv, seg, n, seg_len, 0)
        hist = jnp.pad(prev_conv, ((0, 0), (SUBLANES - (CONV_W - 1), 0), (0, 0))).reshape(n_seg * SUBLANES, d_conv)
        hmap = lambda i, j: (i, j)
        bc, utail = _conv(proj, b_col, c_col, x_col, hist, hmap, jnp.ones_like(hist), hmap,
                          w["conv_w"], w["conv_b"], seg_len, 0)
        k_new = proj[:, d_q:d_q + d_kv].reshape(n_seg, seg_len, d_kv)
        v_new = proj[:, d_q + d_kv:d_q + 2 * d_kv].reshape(n_seg, seg_len, d_kv)
        new_k = jnp.concatenate([prev_k.reshape(n_seg, WINDOW, d_kv), k_new], axis=1)[:, -WINDOW:]
        new_v = jnp.concatenate([prev_v.reshape(n_seg, WINDOW, d_kv), v_new], axis=1)[:, -WINDOW:]
    new_conv = utail.reshape(n_seg, -1, SUBLANES, d_conv)[:, -1, SUBLANES - (CONV_W - 1):]

    merged = _merge(attn, bc, w["w_proj_a"], w["w_proj_b"], proj, ga_col, gb_col)
    h, xn = _outproj(merged, x, w["w_out"], w["norm_ffn_g"])
    xt, s2, e2, th, e1 = _route(xn, w["peer_w_query"], w["peer_keys1"], w["peer_keys2"])
    y = _peer(xt, s2, e2, th, e1, w["peer_u"], w["peer_vt"], h, norm_final_g, last)
    shape5 = (n_seg, WINDOW, N_KV_HEADS, HEAD_DIM)
    return y, new_k.reshape(shape5), new_v.reshape(shape5), new_conv


def kernel(x_prompt, x_sample, state_attn_k, state_attn_v, state_conv, norm_mix_g, w_in, attn_sinks, conv_w,
           conv_b, w_proj_a, w_proj_b, w_out, norm_ffn_g, peer_w_query, peer_keys1, peer_keys2, peer_u, peer_v,
           norm_final_g):
    depth = w_in.shape[0]
    bp, sp, d = x_prompt.shape
    bs, ss, _ = x_sample.shape
    yp = x_prompt.reshape(bp * sp, d)
    ys = x_sample.reshape(bs * ss, d)
    outs = [[] for _ in range(6)]
    for l in range(depth):
        w = dict(
            norm_mix_g=norm_mix_g[l], w_in=w_in[l].astype(BF16), attn_sinks=attn_sinks[l],
            conv_w=conv_w[l], conv_b=conv_b[l], w_proj_a=w_proj_a[l].astype(BF16),
            w_proj_b=w_proj_b[l].astype(BF16), w_out=w_out[l].astype(BF16), norm_ffn_g=norm_ffn_g[l],
            peer_w_query=peer_w_query[l].astype(BF16), peer_keys1=peer_keys1[l].astype(BF16),
            peer_keys2=peer_keys2[l].astype(BF16), peer_u=peer_u[l].astype(BF16),
            peer_vt=_transpose_cast(peer_v[l], BF16),
        )
        last = l == depth - 1
        yp, k1, v1, c1 = _layer(yp, sp, None, None, None, w, norm_final_g, last)
        ys, k2, v2, c2 = _layer(ys, ss, state_attn_k[l], state_attn_v[l], state_conv[l], w, norm_final_g, last)
        for lst, val in zip(outs, (k1, v1, c1, k2, v2, c2)):
            lst.append(val)
    return (yp.reshape(bp, sp, d), ys.reshape(bs, ss, d)) + tuple(jnp.stack(o) for o in outs)
```

```python
import functools
import math

import jax
import jax.numpy as jnp
from jax import lax
from jax.experimental import pallas as pl
from jax.experimental.pallas import tpu as pltpu

F32 = jnp.float32
BF16 = jnp.bfloat16

CHUNK = 64
N_HEADS = 32
N_KV_HEADS = 4
GROUP = N_HEADS // N_KV_HEADS
HEAD_DIM = 64
WINDOW = 128
BAND = WINDOW + CHUNK
CONV_W = 3
PEER_HEADS = 8
N_KEYS = 128
D_HALF = 128
TOPK = 16
EPS = 1e-6
NEG_INF = -1e30

SUBLANES = 8
LANES = 128
VMEM_LIMIT = 60 * 1024 * 1024

_NT = (((1,), (1,)), ((), ()))


def _params(*sem, flags=None):
    return pltpu.CompilerParams(dimension_semantics=sem, vmem_limit_bytes=VMEM_LIMIT, flags=flags)


def _tile(n, pref):
    t = min(n, pref)
    while n % t:
        t //= 2
    return t


def _oddeven_merge_sort(n):
    pairs = []

    def merge(lo, m, r):
        step = r * 2
        if step < m:
            merge(lo, m, step)
            merge(lo + r, m, step)
            for i in range(lo + r, lo + m - r, step):
                pairs.append((i, i + r))
        else:
            pairs.append((lo, lo + r))

    def sort(lo, m):
        if m > 1:
            half = m // 2
            sort(lo, half)
            sort(lo + half, half)
            merge(lo, m, 1)

    sort(0, n)
    return pairs


def _bitonic_merge(n):
    pairs = []
    d = n // 2
    while d >= 1:
        for i in range(n):
            if (i & d) == 0:
                pairs.append((i, i + d))
        d //= 2
    return pairs


_SORT16 = _oddeven_merge_sort(TOPK)
_BITONIC16 = _bitonic_merge(TOPK)


def _apply_net(pairs, xs):
    xs = list(xs)
    for i, j in pairs:
        a, b = xs[i], xs[j]
        xs[i] = jnp.maximum(a, b)
        xs[j] = jnp.minimum(a, b)
    return xs


def _top16_bitonic(a, b):
    return [jnp.maximum(a[i], b[TOPK - 1 - i]) for i in range(TOPK)]


def _inproj_kernel(x_ref, g_ref, w_ref, o_ref, xn_ref):
    @pl.when(pl.program_id(1) == 0)
    def _():
        x = x_ref[...]
        r = lax.rsqrt(jnp.mean(x * x, axis=-1, keepdims=True) + EPS)
        xn_ref[...] = (x * r * g_ref[...]).astype(BF16)

    o_ref[...] = jnp.dot(xn_ref[...], w_ref[...], preferred_element_type=F32)


def _inproj(x, g, w):
    n, d = x.shape
    d_in = w.shape[1]
    tm = _tile(n, 1024)
    tn = 512
    return pl.pallas_call(
        _inproj_kernel,
        grid=(n // tm, d_in // tn),
        in_specs=[
            pl.BlockSpec((tm, d), lambda i, j: (i, 0)),
            pl.BlockSpec((1, d), lambda i, j: (0, 0)),
            pl.BlockSpec((d, tn), lambda i, j: (0, j)),
        ],
        out_specs=pl.BlockSpec((tm, tn), lambda i, j: (i, j)),
        out_shape=jax.ShapeDtypeStruct((n, d_in), F32),
        scratch_shapes=[pltpu.VMEM((tm, d), BF16)],
        compiler_params=_params("parallel", "arbitrary"),
        name="inproj",
    )(x, g.reshape(1, d), w)


def _attn_kernel(sink_ref, q_ref, k_ref, v_ref, pk_ref, pv_ref, o_ref, kall, vall, *,
                 n_chunks, tiles_per_seg):
    kall[0:WINDOW, :] = pk_ref[...].astype(BF16)
    vall[0:WINDOW, :] = pv_ref[...].astype(BF16)
    kall[WINDOW:, :] = k_ref[...].astype(BF16)
    vall[WINDOW:, :] = v_ref[...].astype(BF16)
    nq = GROUP * CHUNK
    if tiles_per_seg:
        band_chunk = lax.broadcasted_iota(jnp.int32, (BAND, nq), 0) // CHUNK
    q_group = lax.broadcasted_iota(jnp.int32, (1, nq), 1) // CHUNK
    sink_rows = []
    for kh in range(N_KV_HEADS):
        row = jnp.zeros((1, nq), F32)
        for g in range(GROUP):
            row = jnp.where(q_group == g, sink_ref[kh * GROUP + g], row)
        sink_rows.append(row)

    def chunk_pair(masked, i, carry):
        for u in range(unroll):
            chunk_body(masked, i * unroll + u)
        return carry

    def chunk_body(masked, c):
        r0 = pl.multiple_of(c * CHUNK, CHUNK)
        qc = q_ref[pl.ds(r0, CHUNK), :] * (HEAD_DIM ** -0.5)
        kb = kall[pl.ds(r0, BAND), :]
        vb = vall[pl.ds(r0, BAND), :]
        if masked:
            valid = (c + band_chunk) >= (WINDOW // CHUNK)
        outs = []
        for kh in range(N_KV_HEADS):
            k_h = kb[:, kh * HEAD_DIM:(kh + 1) * HEAD_DIM]
            v_h = vb[:, kh * HEAD_DIM:(kh + 1) * HEAD_DIM]
            q_h = jnp.concatenate(
                [qc[:, (kh * GROUP + g) * HEAD_DIM:(kh * GROUP + g + 1) * HEAD_DIM] for g in range(GROUP)],
                axis=0).astype(BF16)
            s = lax.dot_general(k_h, q_h, _NT, preferred_element_type=F32)
            if masked:
                s = jnp.where(valid, s, NEG_INF)
            sink = sink_rows[kh]
            m = jnp.maximum(jnp.max(s, axis=0, keepdims=True), sink)
            p = jnp.exp(s - m)
            probs = p / (jnp.sum(p, axis=0, keepdims=True) + jnp.exp(sink - m))
            o_t = lax.dot_general(v_h, probs.astype(BF16), (((0,), (0,)), ((), ())),
                                  preferred_element_type=F32)
            o = o_t.T
            outs += [o[g * CHUNK:(g + 1) * CHUNK, :] for g in range(GROUP)]
        o_ref[pl.ds(r0, CHUNK), :] = jnp.concatenate(outs, axis=1).astype(BF16)

    unroll = 2 if n_chunks % 2 == 0 else 1
    if tiles_per_seg:
        first = pl.program_id(0) % tiles_per_seg == 0

        @pl.when(first)
        def _():
            lax.fori_loop(0, n_chunks // unroll, functools.partial(chunk_pair, True), 0)

        @pl.when(jnp.logical_not(first))
        def _():
            lax.fori_loop(0, n_chunks // unroll, functools.partial(chunk_pair, False), 0)
    else:
        lax.fori_loop(0, n_chunks // unroll, functools.partial(chunk_pair, False), 0)


def _attention(sinks, q_arr, q_col, k_arr, k_col, v_arr, v_col, pk_arr, pk_map, pv_arr, pv_map,
               n, tt, tiles_per_seg):
    d_q = N_HEADS * HEAD_DIM
    d_kv = N_KV_HEADS * HEAD_DIM
    body = functools.partial(_attn_kernel, n_chunks=tt // CHUNK, tiles_per_seg=tiles_per_seg)
    return pl.pallas_call(
        body,
        grid=(n // tt,),
        in_specs=[
            pl.BlockSpec(memory_space=pltpu.SMEM),
            pl.BlockSpec((tt, d_q), lambda i: (i, q_col)),
            pl.BlockSpec((tt, d_kv), lambda i: (i, k_col)),
            pl.BlockSpec((tt, d_kv), lambda i: (i, v_col)),
            pl.BlockSpec((WINDOW, d_kv), pk_map),
            pl.BlockSpec((WINDOW, d_kv), pv_map),
        ],
        out_specs=pl.BlockSpec((tt, d_q), lambda i: (i, 0)),
        out_shape=jax.ShapeDtypeStruct((n, d_q), BF16),
        scratch_shapes=[pltpu.VMEM((WINDOW + tt, d_kv), BF16), pltpu.VMEM((WINDOW + tt, d_kv), BF16)],
        compiler_params=_params("arbitrary"),
        name="attention",
    )(sinks, q_arr, k_arr, v_arr, pk_arr, pv_arr)


def _conv_kernel(b_ref, c_ref, x_ref, hc_ref, hx_ref, w_ref, cb_ref, bc_ref, ut_ref, *, tiles_per_seg):
    u = c_ref[...] * x_ref[...]
    uh = hc_ref[...] * hx_ref[...]
    if tiles_per_seg:
        uh = jnp.where(pl.program_id(0) % tiles_per_seg == 0, 0.0, uh)
    row = lax.broadcasted_iota(jnp.int32, u.shape, 0)
    um1 = jnp.where(row == 0, uh[7:8, :], pltpu.roll(u, 1, axis=0))
    um2 = jnp.where(row == 0, uh[6:7, :], jnp.where(row == 1, uh[7:8, :], pltpu.roll(u, 2, axis=0)))
    conv = cb_ref[...] + w_ref[0:1, :] * um2
    conv = conv + w_ref[1:2, :] * um1
    conv = conv + w_ref[2:3, :] * u
    bc_ref[...] = (b_ref[...] * conv).astype(BF16)
    ut_ref[...] = u[u.shape[0] - SUBLANES:, :]


def _conv(proj, b_col, c_col, x_col, hc_arr, hc_map, hx_arr, hx_map, conv_w, conv_b, tm, tiles_per_seg):
    n = proj.shape[0]
    d_conv = conv_w.shape[1]
    tn = 512
    nh = d_conv // tn
    body = functools.partial(_conv_kernel, tiles_per_seg=tiles_per_seg)
    return pl.pallas_call(
        body,
        grid=(n // tm, nh),
        in_specs=[
            pl.BlockSpec((tm, tn), lambda i, j: (i, b_col + j)),
            pl.BlockSpec((tm, tn), lambda i, j: (i, c_col + j)),
            pl.BlockSpec((tm, tn), lambda i, j: (i, x_col + j)),
            pl.BlockSpec((SUBLANES, tn), hc_map),
            pl.BlockSpec((SUBLANES, tn), hx_map),
            pl.BlockSpec((CONV_W, tn), lambda i, j: (0, j)),
            pl.BlockSpec((1, tn), lambda i, j: (0, j)),
        ],
        out_specs=[
            pl.BlockSpec((tm, tn), lambda i, j: (i, j)),
            pl.BlockSpec((SUBLANES, tn), lambda i, j: (i, j)),
        ],
        out_shape=[
            jax.ShapeDtypeStruct((n, d_conv), BF16),
            jax.ShapeDtypeStruct((n // tm * SUBLANES, d_conv), F32),
        ],
        compiler_params=_params("arbitrary", "arbitrary"),
        name="conv",
    )(proj, proj, proj, hc_arr, hx_arr, conv_w, conv_b.reshape(1, d_conv))


def _merge_kernel(a_ref, bc_ref, wa_ref, wb_ref, ga_ref, gb_ref, o_ref):
    ya = jnp.dot(a_ref[...], wa_ref[...], preferred_element_type=F32)
    yb = jnp.dot(bc_ref[...], wb_ref[...], preferred_element_type=F32)
    o_ref[...] = (jax.nn.sigmoid(ga_ref[...]) * ya + jax.nn.sigmoid(gb_ref[...]) * yb).astype(BF16)


def _merge(attn, bc, wa, wb, proj, ga_col, gb_col):
    n, d_q = attn.shape
    d_conv = bc.shape[1]
    d = wa.shape[1]
    tm = _tile(n, 1024)
    tn = 512
    return pl.pallas_call(
        _merge_kernel,
        grid=(n // tm, d // tn),
        in_specs=[
            pl.BlockSpec((tm, d_q), lambda i, j: (i, 0)),
            pl.BlockSpec((tm, d_conv), lambda i, j: (i, 0)),
            pl.BlockSpec((d_q, tn), lambda i, j: (0, j)),
            pl.BlockSpec((d_conv, tn), lambda i, j: (0, j)),
            pl.BlockSpec((tm, tn), lambda i, j: (i, ga_col + j)),
            pl.BlockSpec((tm, tn), lambda i, j: (i, gb_col + j)),
        ],
        out_specs=pl.BlockSpec((tm, tn), lambda i, j: (i, j)),
        out_shape=jax.ShapeDtypeStruct((n, d), BF16),
        compiler_params=_params("parallel", "arbitrary"),
        name="merge",
    )(attn, bc, wa, wb, proj, proj)


def _outproj_kernel(m_ref, x_ref, w_ref, g_ref, h_ref, xn_ref):
    h = x_ref[...] + jnp.dot(m_ref[...], w_ref[...], preferred_element_type=F32)
    h_ref[...] = h
    r = lax.rsqrt(jnp.mean(h * h, axis=-1, keepdims=True) + EPS)
    xn_ref[...] = (h * r * g_ref[...]).astype(BF16)


def _outproj(merged, x, w, g):
    n, d = x.shape
    tm = _tile(n, 512)
    return pl.pallas_call(
        _outproj_kernel,
        grid=(n // tm,),
        in_specs=[
            pl.BlockSpec((tm, d), lambda i: (i, 0)),
            pl.BlockSpec((tm, d), lambda i: (i, 0)),
            pl.BlockSpec((d, d), lambda i: (0, 0)),
            pl.BlockSpec((1, d), lambda i: (0, 0)),
        ],
        out_specs=[pl.BlockSpec((tm, d), lambda i: (i, 0)), pl.BlockSpec((tm, d), lambda i: (i, 0))],
        out_shape=[jax.ShapeDtypeStruct((n, d), F32), jax.ShapeDtypeStruct((n, d), BF16)],
        compiler_params=_params("parallel"),
        name="outproj",
    )(merged, x, w, g.reshape(1, d))


def _pair_threshold(v1, v2):
    c = lambda i, j: v1[i] + v2[j]
    row0 = [c(0, j) for j in range(TOPK)]
    grp1 = [c(1, j) for j in range(8)] + [c(2, j) for j in range(5)] + [c(3, j) for j in range(3)]
    grp2 = ([c(3, 3)] + [c(4, j) for j in range(3)] + [c(5, 0), c(5, 1), c(6, 0), c(6, 1), c(7, 0), c(7, 1)]
            + [c(i, 0) for i in range(8, 14)])
    top = _apply_net(_BITONIC16, _top16_bitonic(row0, _apply_net(_SORT16, grp1)))
    top = _apply_net(_BITONIC16, _top16_bitonic(top, _apply_net(_SORT16, grp2)))
    top[TOPK - 1] = jnp.maximum(top[TOPK - 1], c(14, 0))
    top[TOPK - 2] = jnp.maximum(top[TOPK - 2], c(15, 0))
    tau = functools.reduce(jnp.minimum, top)
    m = row0[0]
    z = functools.reduce(lambda a, b: a + b, [jnp.exp(t - m) for t in top])
    return tau, z


TOP_R = 3
GEN_J = TOPK // (TOP_R + 1)


def _route_kernel(xn_ref, wq_ref, k1_ref, k2_ref, xt_ref, s2_ref, e2_ref, th_ref, e1_ref,
                  q_scr, s_scr, top_scr):
    tb = xn_ref.shape[0]
    q_scr[...] = jnp.dot(xn_ref[...], wq_ref[...], preferred_element_type=F32).astype(BF16)
    xt_ref[...] = xn_ref[...].astype(F32).T.astype(BF16)
    for h in range(PEER_HEADS):
        rows = slice(h * N_KEYS, (h + 1) * N_KEYS)
        q1 = q_scr[:, 2 * h * D_HALF:(2 * h + 1) * D_HALF]
        q2 = q_scr[:, (2 * h + 1) * D_HALF:(2 * h + 2) * D_HALF]
        s_scr[0, rows, :] = lax.dot_general(k1_ref[h], q1, _NT, preferred_element_type=F32)
        s_scr[1, rows, :] = lax.dot_general(k2_ref[h], q2, _NT, preferred_element_type=F32)

    def lane_block(lb, carry):
        cols = pl.ds(pl.multiple_of(lb * LANES, LANES), LANES)
        for side in range(2):
            for h in range(PEER_HEADS):
                lst = [s_scr[side, pl.ds(h * N_KEYS + SUBLANES * r, SUBLANES), cols]
                       for r in range(N_KEYS // SUBLANES)]
                lst = _apply_net(_SORT16, lst)
                for d in (4, 2, 1):
                    other = [pltpu.roll(x, d, axis=0) for x in lst]
                    lst = _apply_net(_BITONIC16, _top16_bitonic(lst, other))
                for i in range(TOPK):
                    top_scr[side, pl.ds(i * PEER_HEADS + h, 1), cols] = lst[i][0:1, :]
        v1 = [top_scr[0, pl.ds(i * PEER_HEADS, PEER_HEADS), cols] for i in range(TOPK)]
        v2 = [top_scr[1, pl.ds(i * PEER_HEADS, PEER_HEADS), cols] for i in range(TOPK)]
        tau, z = _pair_threshold(v1, v2)
        reach = []
        for j in range(GEN_J):
            t = jnp.full_like(tau, jnp.inf)
            for r in range(TOPK // (j + 1)):
                t = jnp.where(v1[r] + v2[j] >= tau, v1[r], t)
            reach.append(t)
        floor_top = []
        for r in range(TOP_R):
            t = jnp.full_like(tau, jnp.inf)
            for j in range(TOPK // (r + 1)):
                t = jnp.where(v1[r] + v2[j] >= tau, v2[j], t)
            floor_top.append(t)
        for h in range(PEER_HEADS):
            rows = slice(h * N_KEYS, (h + 1) * N_KEYS)
            s1 = s_scr[0, rows, cols]
            s2 = s_scr[1, rows, cols]
            th = jnp.full_like(s1, jnp.inf)
            for j in range(GEN_J):
                th = jnp.where(s1 >= reach[j][h:h + 1, :], v2[j][h:h + 1, :], th)
            for r in reversed(range(TOP_R)):
                th = jnp.where(s1 >= v1[r][h:h + 1, :], floor_top[r][h:h + 1, :], th)
            th_ref[rows, cols] = th
            e1_ref[rows, cols] = jnp.exp(s1 - v1[0][h:h + 1, :])
            s2_ref[rows, cols] = s2
            e2_ref[rows, cols] = jnp.exp(s2 - v2[0][h:h + 1, :]) / z[h:h + 1, :]
        return carry

    lax.fori_loop(0, tb // LANES, lane_block, 0)


def _route(xn, wq, k1, k2):
    n, d = xn.shape
    dq = wq.shape[1]
    tb = _tile(n, 512)
    rows = PEER_HEADS * N_KEYS
    tok = lambda i: (0, i)
    return pl.pallas_call(
        _route_kernel,
        grid=(n // tb,),
        in_specs=[
            pl.BlockSpec((tb, d), lambda i: (i, 0)),
            pl.BlockSpec((d, dq), lambda i: (0, 0)),
            pl.BlockSpec((PEER_HEADS, N_KEYS, D_HALF), lambda i: (0, 0, 0)),
            pl.BlockSpec((PEER_HEADS, N_KEYS, D_HALF), lambda i: (0, 0, 0)),
        ],
        out_specs=[pl.BlockSpec((d, tb), tok)] + [pl.BlockSpec((rows, tb), tok)] * 4,
        out_shape=[jax.ShapeDtypeStruct((d, n), BF16)] + [jax.ShapeDtypeStruct((rows, n), F32)] * 4,
        scratch_shapes=[
            pltpu.VMEM((tb, dq), BF16),
            pltpu.VMEM((2, rows, tb), F32),
            pltpu.VMEM((2, TOPK * PEER_HEADS, tb), F32),
        ],
        compiler_params=_params("parallel"),
        name="peer_route",
    )(xn, wq, k1, k2)


def _peer_kernel(xt_ref, s2_ref, e2_ref, th_ref, e1_ref, u_ref, vt_ref, o_ref, h0, h1, a0, a1, *, nblk):
    eb, tb = h0.shape
    d = o_ref.shape[0]
    n_i1 = eb // N_KEYS
    rb = 2 * SUBLANES
    n_lb = tb // LANES
    hrows = eb // n_lb
    orows = d // n_lb
    s = pl.program_id(0)

    @pl.when(s == 0)
    def _():
        for ref in (h0, h1, a0, a1):
            ref[...] = jnp.zeros_like(ref)

    @pl.when((s < 2) | ((s - 2) % nblk == 0))
    def _():
        o_ref[...] = jnp.zeros_like(o_ref)

    def step_part(h_new, h_prev, a_prev, a_old, lb):
        hr = pl.ds(lb * hrows, hrows)
        h_new[hr, :] = jnp.dot(u_ref[hr, :], xt_ref[...], preferred_element_type=F32)

        cols = pl.ds(lb * LANES, LANES)
        for r0 in range(0, N_KEYS, rb):
            gate = [None] * n_i1
            for h in range(PEER_HEADS):
                s2 = s2_ref[h, r0:r0 + rb, cols]
                e2 = e2_ref[h, r0:r0 + rb, cols]
                for a in range(n_i1):
                    t = jnp.where(s2 >= th_ref[h, a:a + 1, cols], e2, 0.0) * e1_ref[h, a:a + 1, cols]
                    gate[a] = t if gate[a] is None else gate[a] + t
            for a in range(n_i1):
                hid = h_prev[a * N_KEYS + r0:a * N_KEYS + r0 + rb, cols]
                act = 0.5 * hid * (1.0 + lax.erf(hid * (1.0 / math.sqrt(2.0))))
                a_prev[a * N_KEYS + r0:a * N_KEYS + r0 + rb, cols] = (act * gate[a]).astype(BF16)

        orow = pl.ds(lb * orows, orows)
        o_ref[orow, :] += jnp.dot(vt_ref[orow, :], a_old[...], preferred_element_type=F32)

    @pl.when(s % 2 == 0)
    def _():
        for lb in range(n_lb):
            step_part(h0, h1, a1, a0, lb)

    @pl.when(s % 2 == 1)
    def _():
        for lb in range(n_lb):
            step_part(h1, h0, a0, a1, lb)


def _peer(xt, s2, e2, th, e1, u, vt):
    d, n = xt.shape
    n_exp = u.shape[0]
    tb = _tile(n, 512)
    eb = 1024
    n_i1 = eb // N_KEYS
    nblk = n_exp // eb
    n_tiles = n // tb
    last = n_tiles * nblk - 1
    r3 = lambda a: a.reshape(PEER_HEADS, N_KEYS, n)
    step = lambda s, lag: jnp.clip(s - lag, 0, last)
    tile = lambda s, lag: step(s, lag) // nblk
    blk = lambda s, lag: step(s, lag) % nblk
    return pl.pallas_call(
        functools.partial(_peer_kernel, nblk=nblk),
        grid=(n_tiles * nblk + 2,),
        in_specs=[
            pl.BlockSpec((d, tb), lambda s: (0, tile(s, 0))),
            pl.BlockSpec((PEER_HEADS, N_KEYS, tb), lambda s: (0, 0, tile(s, 1))),
            pl.BlockSpec((PEER_HEADS, N_KEYS, tb), lambda s: (0, 0, tile(s, 1))),
            pl.BlockSpec((PEER_HEADS, n_i1, tb), lambda s: (0, blk(s, 1), tile(s, 1))),
            pl.BlockSpec((PEER_HEADS, n_i1, tb), lambda s: (0, blk(s, 1), tile(s, 1))),
            pl.BlockSpec((eb, d), lambda s: (blk(s, 0), 0)),
            pl.BlockSpec((d, eb), lambda s: (0, blk(s, 2))),
        ],
        out_specs=pl.BlockSpec((d, tb), lambda s: (0, tile(s, 2))),
        out_shape=jax.ShapeDtypeStruct((d, n), F32),
        scratch_shapes=[pltpu.VMEM((eb, tb), F32)] * 2 + [pltpu.VMEM((eb, tb), BF16)] * 2,
        compiler_params=_params("arbitrary"),
        name="peer_dense",
    )(xt, r3(s2), r3(e2), r3(th), r3(e1), u, vt)


def _final_kernel(h_ref, pt_ref, g_ref, y_ref, *, normalize):
    y = h_ref[...] + pt_ref[...].T
    if normalize:
        r = lax.rsqrt(jnp.mean(y * y, axis=-1, keepdims=True) + EPS)
        y = y * r * g_ref[...]
    y_ref[...] = y


def _final(h, pt, g, normalize):
    n, d = h.shape
    tm = _tile(n, 512)
    return pl.pallas_call(
        functools.partial(_final_kernel, normalize=normalize),
        grid=(n // tm,),
        in_specs=[
            pl.BlockSpec((tm, d), lambda i: (i, 0)),
            pl.BlockSpec((d, tm), lambda i: (0, i)),
            pl.BlockSpec((1, d), lambda i: (0, 0)),
        ],
        out_specs=pl.BlockSpec((tm, d), lambda i: (i, 0)),
        out_shape=jax.ShapeDtypeStruct((n, d), F32),
        compiler_params=_params("parallel"),
        name="final",
    )(h, pt, g.reshape(1, d))


def _transpose_kernel(x_ref, o_ref):
    o_ref[...] = x_ref[...].T


def _transpose(x):
    r, c = x.shape
    tr = _tile(r, 512)
    return pl.pallas_call(
        _transpose_kernel,
        grid=(r // tr,),
        in_specs=[pl.BlockSpec((tr, c), lambda i: (i, 0))],
        out_specs=pl.BlockSpec((c, tr), lambda i: (0, i)),
        out_shape=jax.ShapeDtypeStruct((c, r), x.dtype),
        compiler_params=_params("parallel"),
        name="transpose",
    )(x)


def _layer(x, seg_len, prev_k, prev_v, prev_conv, w, norm_final_g, last):
    n, d = x.shape
    n_seg = n // seg_len
    d_q = N_HEADS * HEAD_DIM
    d_kv = N_KV_HEADS * HEAD_DIM
    d_conv = w["conv_w"].shape[1]
    proj = _inproj(x, w["norm_mix_g"], w["w_in"])
    k_col, v_col = d_q // d_kv, d_q // d_kv + 1
    o_b = d_q + 2 * d_kv
    b_col, c_col, x_col = o_b // 512, (o_b + d_conv) // 512, (o_b + 2 * d_conv) // 512
    ga_col, gb_col = (o_b + 3 * d_conv) // 512, (o_b + 3 * d_conv + d) // 512
    assert o_b % 512 == 0 and d_conv % 512 == 0 and d % 512 == 0 and d_q % d_kv == 0

    if prev_k is None:
        tt = _tile(seg_len, 512)
        assert tt % WINDOW == 0
        tps = seg_len // tt
        halo = lambda col: (lambda i: (jnp.maximum(i * (tt // WINDOW) - 1, 0), col))
        attn = _attention(w["attn_sinks"], proj, 0, proj, k_col, proj, v_col,
                          proj, halo(k_col), proj, halo(v_col), n, tt, tps)
        tc = tt
        chalo = lambda col: (lambda i, j: (jnp.maximum(i * (tc // SUBLANES) - 1, 0), col + j))
        bc, utail = _conv(proj, b_col, c_col, x_col, proj, chalo(c_col), proj, chalo(x_col),
                          w["conv_w"], w["conv_b"], tc, seg_len // tc)
        new_k = proj.reshape(n_seg, seg_len, -1)[:, -WINDOW:, d_q:d_q + d_kv]
        new_v = proj.reshape(n_seg, seg_len, -1)[:, -WINDOW:, d_q + d_kv:d_q + 2 * d_kv]
    else:
        assert seg_len == CHUNK
        pk = prev_k.reshape(n_seg * WINDOW, d_kv)
        pv = prev_v.reshape(n_seg * WINDOW, d_kv)
        seg = lambda i: (i, 0)
        attn = _attention(w["attn_sinks"], proj, 0, proj, k_col, proj, v_col,
                          pk, seg, pv, seg, n, seg_len, 0)
        hist = jnp.pad(prev_conv, ((0, 0), (SUBLANES - (CONV_W - 1), 0), (0, 0))).reshape(n_seg * SUBLANES, d_conv)
        hmap = lambda i, j: (i, j)
        bc, utail = _conv(proj, b_col, c_col, x_col, hist, hmap, jnp.ones_like(hist), hmap,
                          w["conv_w"], w["conv_b"], seg_len, 0)
        k_new = proj[:, d_q:d_q + d_kv].reshape(n_seg, seg_len, d_kv)
        v_new = proj[:, d_q + d_kv:d_q + 2 * d_kv].reshape(n_seg, seg_len, d_kv)
        new_k = jnp.concatenate([prev_k.reshape(n_seg, WINDOW, d_kv), k_new], axis=1)[:, -WINDOW:]
        new_v = jnp.concatenate([prev_v.reshape(n_seg, WINDOW, d_kv), v_new], axis=1)[:, -WINDOW:]
    new_conv = utail.reshape(n_seg, -1, SUBLANES, d_conv)[:, -1, SUBLANES - (CONV_W - 1):]

    merged = _merge(attn, bc, w["w_proj_a"], w["w_proj_b"], proj, ga_col, gb_col)
    h, xn = _outproj(merged, x, w["w_out"], w["norm_ffn_g"])
    xt, s2, e2, th, e1 = _route(xn, w["peer_w_query"], w["peer_keys1"], w["peer_keys2"])
    pt = _peer(xt, s2, e2, th, e1, w["peer_u"], w["peer_vt"])
    y = _final(h, pt, norm_final_g, last)
    shape5 = (n_seg, WINDOW, N_KV_HEADS, HEAD_DIM)
    return y, new_k.reshape(shape5), new_v.reshape(shape5), new_conv


def kernel(x_prompt, x_sample, state_attn_k, state_attn_v, state_conv, norm_mix_g, w_in, attn_sinks, conv_w,
           conv_b, w_proj_a, w_proj_b, w_out, norm_ffn_g, peer_w_query, peer_keys1, peer_keys2, peer_u, peer_v,
           norm_final_g):
    depth = w_in.shape[0]
    bp, sp, d = x_prompt.shape
    bs, ss, _ = x_sample.shape
    yp = x_prompt.reshape(bp * sp, d)
    ys = x_sample.reshape(bs * ss, d)
    outs = [[] for _ in range(6)]
    for l in range(depth):
        w = dict(
            norm_mix_g=norm_mix_g[l], w_in=w_in[l].astype(BF16), attn_sinks=attn_sinks[l],
            conv_w=conv_w[l], conv_b=conv_b[l], w_proj_a=w_proj_a[l].astype(BF16),
            w_proj_b=w_proj_b[l].astype(BF16), w_out=w_out[l].astype(BF16), norm_ffn_g=norm_ffn_g[l],
            peer_w_query=peer_w_query[l].astype(BF16), peer_keys1=peer_keys1[l].astype(BF16),
            peer_keys2=peer_keys2[l].astype(BF16), peer_u=peer_u[l],
            peer_vt=_transpose(peer_v[l]),
        )
        last = l == depth - 1
        yp, k1, v1, c1 = _layer(yp, sp, None, None, None, w, norm_final_g, last)
        ys, k2, v2, c2 = _layer(ys, ss, state_attn_k[l], state_attn_v[l], state_conv[l], w, norm_final_g, last)
        for lst, val in zip(outs, (k1, v1, c1, k2, v2, c2)):
            lst.append(val)
    return (yp.reshape(bp, sp, d), ys.reshape(bs, ss, d)) + tuple(jnp.stack(o) for o in outs)
```

```python
import functools
import math

import jax
import jax.numpy as jnp
from jax import lax
from jax.experimental import pallas as pl
from jax.experimental.pallas import tpu as pltpu

F32 = jnp.float32
BF16 = jnp.bfloat16

CHUNK = 64
N_HEADS = 32
N_KV_HEADS = 4
GROUP = N_HEADS // N_KV_HEADS
HEAD_DIM = 64
WINDOW = 128
BAND = WINDOW + CHUNK
CONV_W = 3
PEER_HEADS = 8
N_KEYS = 128
D_HALF = 128
TOPK = 16
EPS = 1e-6
NEG_INF = -1e30

SUBLANES = 8
LANES = 128
VMEM_LIMIT = 60 * 1024 * 1024

_NT = (((1,), (1,)), ((), ()))


def _params(*sem, flags=None):
    return pltpu.CompilerParams(dimension_semantics=sem, vmem_limit_bytes=VMEM_LIMIT, flags=flags)


def _tile(n, pref):
    t = min(n, pref)
    while n % t:
        t //= 2
    return t


def _oddeven_merge_sort(n):
    pairs = []

    def merge(lo, m, r):
        step = r * 2
        if step < m:
            merge(lo, m, step)
            merge(lo + r, m, step)
            for i in range(lo + r, lo + m - r, step):
                pairs.append((i, i + r))
        else:
            pairs.append((lo, lo + r))

    def sort(lo, m):
        if m > 1:
            half = m // 2
            sort(lo, half)
            sort(lo + half, half)
            merge(lo, m, 1)

    sort(0, n)
    return pairs


def _bitonic_merge(n):
    pairs = []
    d = n // 2
    while d >= 1:
        for i in range(n):
            if (i & d) == 0:
                pairs.append((i, i + d))
        d //= 2
    return pairs


_SORT16 = _oddeven_merge_sort(TOPK)
_BITONIC16 = _bitonic_merge(TOPK)


def _apply_net(pairs, xs):
    xs = list(xs)
    for i, j in pairs:
        a, b = xs[i], xs[j]
        xs[i] = jnp.maximum(a, b)
        xs[j] = jnp.minimum(a, b)
    return xs


def _top16_bitonic(a, b):
    return [jnp.maximum(a[i], b[TOPK - 1 - i]) for i in range(TOPK)]


def _inproj_kernel(x_ref, g_ref, w_ref, o_ref, xn_ref):
    @pl.when(pl.program_id(1) == 0)
    def _():
        x = x_ref[...]
        r = lax.rsqrt(jnp.mean(x * x, axis=-1, keepdims=True) + EPS)
        xn_ref[...] = (x * r * g_ref[...]).astype(BF16)

    o_ref[...] = jnp.dot(xn_ref[...], w_ref[...], preferred_element_type=F32)


def _inproj(x, g, w):
    n, d = x.shape
    d_in = w.shape[1]
    tm = _tile(n, 2048)
    tn = 512
    return pl.pallas_call(
        _inproj_kernel,
        grid=(n // tm, d_in // tn),
        in_specs=[
            pl.BlockSpec((tm, d), lambda i, j: (i, 0)),
            pl.BlockSpec((1, d), lambda i, j: (0, 0)),
            pl.BlockSpec((d, tn), lambda i, j: (0, j)),
        ],
        out_specs=pl.BlockSpec((tm, tn), lambda i, j: (i, j)),
        out_shape=jax.ShapeDtypeStruct((n, d_in), F32),
        scratch_shapes=[pltpu.VMEM((tm, d), BF16)],
        compiler_params=_params("parallel", "arbitrary"),
        name="inproj",
    )(x, g.reshape(1, d), w)


def _attn_kernel(sink_ref, q_ref, k_ref, v_ref, pk_ref, pv_ref, o_ref, kall, vall, *,
                 n_chunks, tiles_per_seg):
    kall[0:WINDOW, :] = pk_ref[...].astype(BF16)
    vall[0:WINDOW, :] = pv_ref[...].astype(BF16)
    kall[WINDOW:, :] = k_ref[...].astype(BF16)
    vall[WINDOW:, :] = v_ref[...].astype(BF16)
    nq = GROUP * CHUNK
    if tiles_per_seg:
        band_chunk = lax.broadcasted_iota(jnp.int32, (BAND, nq), 0) // CHUNK
    q_group = lax.broadcasted_iota(jnp.int32, (1, nq), 1) // CHUNK
    sink_rows = []
    for kh in range(N_KV_HEADS):
        row = jnp.zeros((1, nq), F32)
        for g in range(GROUP):
            row = jnp.where(q_group == g, sink_ref[kh * GROUP + g], row)
        sink_rows.append(row)

    def chunk_pair(masked, i, carry):
        for u in range(unroll):
            chunk_body(masked, i * unroll + u)
        return carry

    def chunk_body(masked, c):
        r0 = pl.multiple_of(c * CHUNK, CHUNK)
        qc = q_ref[pl.ds(r0, CHUNK), :] * (HEAD_DIM ** -0.5)
        kb = kall[pl.ds(r0, BAND), :]
        vb = vall[pl.ds(r0, BAND), :]
        if masked:
            valid = (c + band_chunk) >= (WINDOW // CHUNK)
        outs = []
        for kh in range(N_KV_HEADS):
            k_h = kb[:, kh * HEAD_DIM:(kh + 1) * HEAD_DIM]
            v_h = vb[:, kh * HEAD_DIM:(kh + 1) * HEAD_DIM]
            q_h = jnp.concatenate(
                [qc[:, (kh * GROUP + g) * HEAD_DIM:(kh * GROUP + g + 1) * HEAD_DIM] for g in range(GROUP)],
                axis=0).astype(BF16)
            s = lax.dot_general(k_h, q_h, _NT, preferred_element_type=F32)
            if masked:
                s = jnp.where(valid, s, NEG_INF)
            sink = sink_rows[kh]
            m = jnp.maximum(jnp.max(s, axis=0, keepdims=True), sink)
            p = jnp.exp(s - m)
            probs = p / (jnp.sum(p, axis=0, keepdims=True) + jnp.exp(sink - m))
            o_t = lax.dot_general(v_h, probs.astype(BF16), (((0,), (0,)), ((), ())),
                                  preferred_element_type=F32)
            o = o_t.T
            outs += [o[g * CHUNK:(g + 1) * CHUNK, :] for g in range(GROUP)]
        o_ref[pl.ds(r0, CHUNK), :] = jnp.concatenate(outs, axis=1).astype(BF16)

    unroll = 4 if n_chunks % 4 == 0 else 1
    if tiles_per_seg:
        first = pl.program_id(0) % tiles_per_seg == 0

        @pl.when(first)
        def _():
            lax.fori_loop(0, n_chunks // unroll, functools.partial(chunk_pair, True), 0)

        @pl.when(jnp.logical_not(first))
        def _():
            lax.fori_loop(0, n_chunks // unroll, functools.partial(chunk_pair, False), 0)
    else:
        lax.fori_loop(0, n_chunks // unroll, functools.partial(chunk_pair, False), 0)


def _attention(sinks, q_arr, q_col, k_arr, k_col, v_arr, v_col, pk_arr, pk_map, pv_arr, pv_map,
               n, tt, tiles_per_seg):
    d_q = N_HEADS * HEAD_DIM
    d_kv = N_KV_HEADS * HEAD_DIM
    body = functools.partial(_attn_kernel, n_chunks=tt // CHUNK, tiles_per_seg=tiles_per_seg)
    return pl.pallas_call(
        body,
        grid=(n // tt,),
        in_specs=[
            pl.BlockSpec(memory_space=pltpu.SMEM),
            pl.BlockSpec((tt, d_q), lambda i: (i, q_col)),
            pl.BlockSpec((tt, d_kv), lambda i: (i, k_col)),
            pl.BlockSpec((tt, d_kv), lambda i: (i, v_col)),
            pl.BlockSpec((WINDOW, d_kv), pk_map),
            pl.BlockSpec((WINDOW, d_kv), pv_map),
        ],
        out_specs=pl.BlockSpec((tt, d_q), lambda i: (i, 0)),
        out_shape=jax.ShapeDtypeStruct((n, d_q), BF16),
        scratch_shapes=[pltpu.VMEM((WINDOW + tt, d_kv), BF16), pltpu.VMEM((WINDOW + tt, d_kv), BF16)],
        compiler_params=_params("arbitrary"),
        name="attention",
    )(sinks, q_arr, k_arr, v_arr, pk_arr, pv_arr)


def _conv_kernel(b_ref, c_ref, x_ref, hc_ref, hx_ref, w_ref, cb_ref, bc_ref, ut_ref, *, tiles_per_seg):
    u = c_ref[...] * x_ref[...]
    uh = hc_ref[...] * hx_ref[...]
    if tiles_per_seg:
        uh = jnp.where(pl.program_id(0) % tiles_per_seg == 0, 0.0, uh)
    row = lax.broadcasted_iota(jnp.int32, u.shape, 0)
    um1 = jnp.where(row == 0, uh[7:8, :], pltpu.roll(u, 1, axis=0))
    um2 = jnp.where(row == 0, uh[6:7, :], jnp.where(row == 1, uh[7:8, :], pltpu.roll(u, 2, axis=0)))
    conv = cb_ref[...] + w_ref[0:1, :] * um2
    conv = conv + w_ref[1:2, :] * um1
    conv = conv + w_ref[2:3, :] * u
    bc_ref[...] = (b_ref[...] * conv).astype(BF16)
    ut_ref[...] = u[u.shape[0] - SUBLANES:, :]


def _conv(proj, b_col, c_col, x_col, hc_arr, hc_map, hx_arr, hx_map, conv_w, conv_b, tm, tiles_per_seg):
    n = proj.shape[0]
    d_conv = conv_w.shape[1]
    tn = 512
    nh = d_conv // tn
    body = functools.partial(_conv_kernel, tiles_per_seg=tiles_per_seg)
    return pl.pallas_call(
        body,
        grid=(n // tm, nh),
        in_specs=[
            pl.BlockSpec((tm, tn), lambda i, j: (i, b_col + j)),
            pl.BlockSpec((tm, tn), lambda i, j: (i, c_col + j)),
            pl.BlockSpec((tm, tn), lambda i, j: (i, x_col + j)),
            pl.BlockSpec((SUBLANES, tn), hc_map),
            pl.BlockSpec((SUBLANES, tn), hx_map),
            pl.BlockSpec((CONV_W, tn), lambda i, j: (0, j)),
            pl.BlockSpec((1, tn), lambda i, j: (0, j)),
        ],
        out_specs=[
            pl.BlockSpec((tm, tn), lambda i, j: (i, j)),
            pl.BlockSpec((SUBLANES, tn), lambda i, j: (i, j)),
        ],
        out_shape=[
            jax.ShapeDtypeStruct((n, d_conv), BF16),
            jax.ShapeDtypeStruct((n // tm * SUBLANES, d_conv), F32),
        ],
        compiler_params=_params("arbitrary", "arbitrary"),
        name="conv",
    )(proj, proj, proj, hc_arr, hx_arr, conv_w, conv_b.reshape(1, d_conv))


def _merge_kernel(a_ref, bc_ref, wa_ref, wb_ref, ga_ref, gb_ref, o_ref):
    ya = jnp.dot(a_ref[...], wa_ref[...], preferred_element_type=F32)
    yb = jnp.dot(bc_ref[...], wb_ref[...], preferred_element_type=F32)
    o_ref[...] = (jax.nn.sigmoid(ga_ref[...]) * ya + jax.nn.sigmoid(gb_ref[...]) * yb).astype(BF16)


def _merge(attn, bc, wa, wb, proj, ga_col, gb_col):
    n, d_q = attn.shape
    d_conv = bc.shape[1]
    d = wa.shape[1]
    tm = _tile(n, 1024)
    tn = 512
    return pl.pallas_call(
        _merge_kernel,
        grid=(n // tm, d // tn),
        in_specs=[
            pl.BlockSpec((tm, d_q), lambda i, j: (i, 0)),
            pl.BlockSpec((tm, d_conv), lambda i, j: (i, 0)),
            pl.BlockSpec((d_q, tn), lambda i, j: (0, j)),
            pl.BlockSpec((d_conv, tn), lambda i, j: (0, j)),
            pl.BlockSpec((tm, tn), lambda i, j: (i, ga_col + j)),
            pl.BlockSpec((tm, tn), lambda i, j: (i, gb_col + j)),
        ],
        out_specs=pl.BlockSpec((tm, tn), lambda i, j: (i, j)),
        out_shape=jax.ShapeDtypeStruct((n, d), BF16),
        compiler_params=_params("parallel", "arbitrary"),
        name="merge",
    )(attn, bc, wa, wb, proj, proj)


def _outproj_kernel(m_ref, x_ref, w_ref, g_ref, h_ref, xn_ref):
    h = x_ref[...] + jnp.dot(m_ref[...], w_ref[...], preferred_element_type=F32)
    h_ref[...] = h
    r = lax.rsqrt(jnp.mean(h * h, axis=-1, keepdims=True) + EPS)
    xn_ref[...] = (h * r * g_ref[...]).astype(BF16)


def _outproj(merged, x, w, g):
    n, d = x.shape
    tm = _tile(n, 512)
    return pl.pallas_call(
        _outproj_kernel,
        grid=(n // tm,),
        in_specs=[
            pl.BlockSpec((tm, d), lambda i: (i, 0)),
            pl.BlockSpec((tm, d), lambda i: (i, 0)),
            pl.BlockSpec((d, d), lambda i: (0, 0)),
            pl.BlockSpec((1, d), lambda i: (0, 0)),
        ],
        out_specs=[pl.BlockSpec((tm, d), lambda i: (i, 0)), pl.BlockSpec((tm, d), lambda i: (i, 0))],
        out_shape=[jax.ShapeDtypeStruct((n, d), F32), jax.ShapeDtypeStruct((n, d), BF16)],
        compiler_params=_params("parallel"),
        name="outproj",
    )(merged, x, w, g.reshape(1, d))


def _pair_threshold(v1, v2):
    c = lambda i, j: v1[i] + v2[j]
    row0 = [c(0, j) for j in range(TOPK)]
    grp1 = [c(1, j) for j in range(8)] + [c(2, j) for j in range(5)] + [c(3, j) for j in range(3)]
    grp2 = ([c(3, 3)] + [c(4, j) for j in range(3)] + [c(5, 0), c(5, 1), c(6, 0), c(6, 1), c(7, 0), c(7, 1)]
            + [c(i, 0) for i in range(8, 14)])
    top = _apply_net(_BITONIC16, _top16_bitonic(row0, _apply_net(_SORT16, grp1)))
    top = _apply_net(_BITONIC16, _top16_bitonic(top, _apply_net(_SORT16, grp2)))
    top[TOPK - 1] = jnp.maximum(top[TOPK - 1], c(14, 0))
    top[TOPK - 2] = jnp.maximum(top[TOPK - 2], c(15, 0))
    tau = functools.reduce(jnp.minimum, top)
    m = row0[0]
    z = functools.reduce(lambda a, b: a + b, [jnp.exp(t - m) for t in top])
    return tau, z


TOP_R = 3
GEN_J = TOPK // (TOP_R + 1)


def _route_kernel(xn_ref, wq_ref, k1_ref, k2_ref, xt_ref, s2_ref, e2_ref, th_ref, e1_ref,
                  q_scr, s_scr, top_scr):
    tb = xn_ref.shape[0]
    q_scr[...] = jnp.dot(xn_ref[...], wq_ref[...], preferred_element_type=F32).astype(BF16)
    xt_ref[...] = xn_ref[...].astype(F32).T.astype(BF16)
    for h in range(PEER_HEADS):
        rows = slice(h * N_KEYS, (h + 1) * N_KEYS)
        q1 = q_scr[:, 2 * h * D_HALF:(2 * h + 1) * D_HALF]
        q2 = q_scr[:, (2 * h + 1) * D_HALF:(2 * h + 2) * D_HALF]
        s_scr[0, rows, :] = lax.dot_general(k1_ref[h], q1, _NT, preferred_element_type=F32)
        s_scr[1, rows, :] = lax.dot_general(k2_ref[h], q2, _NT, preferred_element_type=F32)

    def lane_block(lb, carry):
        cols = pl.ds(pl.multiple_of(lb * LANES, LANES), LANES)
        for side in range(2):
            for h in range(PEER_HEADS):
                lst = [s_scr[side, pl.ds(h * N_KEYS + SUBLANES * r, SUBLANES), cols]
                       for r in range(N_KEYS // SUBLANES)]
                lst = _apply_net(_SORT16, lst)
                for d in (4, 2, 1):
                    other = [pltpu.roll(x, d, axis=0) for x in lst]
                    lst = _apply_net(_BITONIC16, _top16_bitonic(lst, other))
                for i in range(TOPK):
                    top_scr[side, pl.ds(i * PEER_HEADS + h, 1), cols] = lst[i][0:1, :]
        v1 = [top_scr[0, pl.ds(i * PEER_HEADS, PEER_HEADS), cols] for i in range(TOPK)]
        v2 = [top_scr[1, pl.ds(i * PEER_HEADS, PEER_HEADS), cols] for i in range(TOPK)]
        tau, z = _pair_threshold(v1, v2)
        reach = []
        for j in range(GEN_J):
            t = jnp.full_like(tau, jnp.inf)
            for r in range(TOPK // (j + 1)):
                t = jnp.where(v1[r] + v2[j] >= tau, v1[r], t)
            reach.append(t)
        floor_top = []
        for r in range(TOP_R):
            t = jnp.full_like(tau, jnp.inf)
            for j in range(TOPK // (r + 1)):
                t = jnp.where(v1[r] + v2[j] >= tau, v2[j], t)
            floor_top.append(t)
        for h in range(PEER_HEADS):
            rows = slice(h * N_KEYS, (h + 1) * N_KEYS)
            s1 = s_scr[0, rows, cols]
            s2 = s_scr[1, rows, cols]
            th = jnp.full_like(s1, jnp.inf)
            for j in range(GEN_J):
                th = jnp.where(s1 >= reach[j][h:h + 1, :], v2[j][h:h + 1, :], th)
            for r in reversed(range(TOP_R)):
                th = jnp.where(s1 >= v1[r][h:h + 1, :], floor_top[r][h:h + 1, :], th)
            th_ref[rows, cols] = th
            e1_ref[rows, cols] = jnp.exp(s1 - v1[0][h:h + 1, :])
            s2_ref[rows, cols] = s2
            e2_ref[rows, cols] = jnp.exp(s2 - v2[0][h:h + 1, :]) / z[h:h + 1, :]
        return carry

    lax.fori_loop(0, tb // LANES, lane_block, 0)


def _route(xn, wq, k1, k2):
    n, d = xn.shape
    dq = wq.shape[1]
    tb = _tile(n, 512)
    rows = PEER_HEADS * N_KEYS
    tok = lambda i: (0, i)
    return pl.pallas_call(
        _route_kernel,
        grid=(n // tb,),
        in_specs=[
            pl.BlockSpec((tb, d), lambda i: (i, 0)),
            pl.BlockSpec((d, dq), lambda i: (0, 0)),
            pl.BlockSpec((PEER_HEADS, N_KEYS, D_HALF), lambda i: (0, 0, 0)),
            pl.BlockSpec((PEER_HEADS, N_KEYS, D_HALF), lambda i: (0, 0, 0)),
        ],
        out_specs=[pl.BlockSpec((d, tb), tok)] + [pl.BlockSpec((rows, tb), tok)] * 4,
        out_shape=[jax.ShapeDtypeStruct((d, n), BF16)] + [jax.ShapeDtypeStruct((rows, n), F32)] * 4,
        scratch_shapes=[
            pltpu.VMEM((tb, dq), BF16),
            pltpu.VMEM((2, rows, tb), F32),
            pltpu.VMEM((2, TOPK * PEER_HEADS, tb), F32),
        ],
        compiler_params=_params("parallel"),
        name="peer_route",
    )(xn, wq, k1, k2)


def _peer_kernel(xt_ref, s2_ref, e2_ref, th_ref, e1_ref, u_ref, vt_ref, o_ref, h0, h1, a0, a1, *, nblk):
    eb, tb = h0.shape
    d = o_ref.shape[0]
    n_i1 = eb // N_KEYS
    rb = 2 * SUBLANES
    n_lb = tb // LANES
    hrows = eb // n_lb
    orows = d // n_lb
    s = pl.program_id(0)

    @pl.when(s == 0)
    def _():
        for ref in (h0, h1, a0, a1):
            ref[...] = jnp.zeros_like(ref)

    @pl.when((s < 2) | ((s - 2) % nblk == 0))
    def _():
        o_ref[...] = jnp.zeros_like(o_ref)

    def step_part(h_new, h_prev, a_prev, a_old, lb):
        hr = pl.ds(lb * hrows, hrows)
        h_new[hr, :] = jnp.dot(u_ref[hr, :], xt_ref[...], preferred_element_type=F32)

        cols = pl.ds(lb * LANES, LANES)
        for r0 in range(0, N_KEYS, rb):
            gate = [None] * n_i1
            for h in range(PEER_HEADS):
                s2 = s2_ref[h, r0:r0 + rb, cols]
                e2 = e2_ref[h, r0:r0 + rb, cols]
                for a in range(n_i1):
                    t = jnp.where(s2 >= th_ref[h, a:a + 1, cols], e2, 0.0) * e1_ref[h, a:a + 1, cols]
                    gate[a] = t if gate[a] is None else gate[a] + t
            for a in range(n_i1):
                hid = h_prev[a * N_KEYS + r0:a * N_KEYS + r0 + rb, cols]
                act = 0.5 * hid * (1.0 + lax.erf(hid * (1.0 / math.sqrt(2.0))))
                a_prev[a * N_KEYS + r0:a * N_KEYS + r0 + rb, cols] = (act * gate[a]).astype(BF16)

        orow = pl.ds(lb * orows, orows)
        o_ref[orow, :] += jnp.dot(vt_ref[orow, :], a_old[...], preferred_element_type=F32)

    @pl.when(s % 2 == 0)
    def _():
        for lb in range(n_lb):
            step_part(h0, h1, a1, a0, lb)

    @pl.when(s % 2 == 1)
    def _():
        for lb in range(n_lb):
            step_part(h1, h0, a0, a1, lb)


def _peer(xt, s2, e2, th, e1, u, vt):
    d, n = xt.shape
    n_exp = u.shape[0]
    tb = _tile(n, 512)
    eb = 1024
    n_i1 = eb // N_KEYS
    nblk = n_exp // eb
    n_tiles = n // tb
    last = n_tiles * nblk - 1
    r3 = lambda a: a.reshape(PEER_HEADS, N_KEYS, n)
    step = lambda s, lag: jnp.clip(s - lag, 0, last)
    tile = lambda s, lag: step(s, lag) // nblk
    blk = lambda s, lag: step(s, lag) % nblk
    return pl.pallas_call(
        functools.partial(_peer_kernel, nblk=nblk),
        grid=(n_tiles * nblk + 2,),
        in_specs=[
            pl.BlockSpec((d, tb), lambda s: (0, tile(s, 0))),
            pl.BlockSpec((PEER_HEADS, N_KEYS, tb), lambda s: (0, 0, tile(s, 1))),
            pl.BlockSpec((PEER_HEADS, N_KEYS, tb), lambda s: (0, 0, tile(s, 1))),
            pl.BlockSpec((PEER_HEADS, n_i1, tb), lambda s: (0, blk(s, 1), tile(s, 1))),
            pl.BlockSpec((PEER_HEADS, n_i1, tb), lambda s: (0, blk(s, 1), tile(s, 1))),
            pl.BlockSpec((eb, d), lambda s: (blk(s, 0), 0)),
            pl.BlockSpec((d, eb), lambda s: (0, blk(s, 2))),
        ],
        out_specs=pl.BlockSpec((d, tb), lambda s: (0, tile(s, 2))),
        out_shape=jax.ShapeDtypeStruct((d, n), F32),
        scratch_shapes=[pltpu.VMEM((eb, tb), F32)] * 2 + [pltpu.VMEM((eb, tb), BF16)] * 2,
        compiler_params=_params("arbitrary"),
        name="peer_dense",
    )(xt, r3(s2), r3(e2), r3(th), r3(e1), u, vt)


def _final_kernel(h_ref, pt_ref, g_ref, y_ref, *, normalize):
    y = h_ref[...] + pt_ref[...].T
    if normalize:
        r = lax.rsqrt(jnp.mean(y * y, axis=-1, keepdims=True) + EPS)
        y = y * r * g_ref[...]
    y_ref[...] = y


def _final(h, pt, g, normalize):
    n, d = h.shape
    tm = _tile(n, 512)
    return pl.pallas_call(
        functools.partial(_final_kernel, normalize=normalize),
        grid=(n // tm,),
        in_specs=[
            pl.BlockSpec((tm, d), lambda i: (i, 0)),
            pl.BlockSpec((d, tm), lambda i: (0, i)),
            pl.BlockSpec((1, d), lambda i: (0, 0)),
        ],
        out_specs=pl.BlockSpec((tm, d), lambda i: (i, 0)),
        out_shape=jax.ShapeDtypeStruct((n, d), F32),
        compiler_params=_params("parallel"),
        name="final",
    )(h, pt, g.reshape(1, d))


def _transpose_kernel(x_ref, o_ref):
    o_ref[...] = x_ref[...].T


def _transpose(x):
    r, c = x.shape
    tr = _tile(r, 512)
    return pl.pallas_call(
        _transpose_kernel,
        grid=(r // tr,),
        in_specs=[pl.BlockSpec((tr, c), lambda i: (i, 0))],
        out_specs=pl.BlockSpec((c, tr), lambda i: (0, i)),
        out_shape=jax.ShapeDtypeStruct((c, r), x.dtype),
        compiler_params=_params("parallel"),
        name="transpose",
    )(x)


def _layer(x, seg_len, prev_k, prev_v, prev_conv, w, norm_final_g, last):
    n, d = x.shape
    n_seg = n // seg_len
    d_q = N_HEADS * HEAD_DIM
    d_kv = N_KV_HEADS * HEAD_DIM
    d_conv = w["conv_w"].shape[1]
    proj = _inproj(x, w["norm_mix_g"], w["w_in"])
    k_col, v_col = d_q // d_kv, d_q // d_kv + 1
    o_b = d_q + 2 * d_kv
    b_col, c_col, x_col = o_b // 512, (o_b + d_conv) // 512, (o_b + 2 * d_conv) // 512
    ga_col, gb_col = (o_b + 3 * d_conv) // 512, (o_b + 3 * d_conv + d) // 512
    assert o_b % 512 == 0 and d_conv % 512 == 0 and d % 512 == 0 and d_q % d_kv == 0

    if prev_k is None:
        tt = _tile(seg_len, 512)
        assert tt % WINDOW == 0
        tps = seg_len // tt
        halo = lambda col: (lambda i: (jnp.maximum(i * (tt // WINDOW) - 1, 0), col))
        attn = _attention(w["attn_sinks"], proj, 0, proj, k_col, proj, v_col,
                          proj, halo(k_col), proj, halo(v_col), n, tt, tps)
        tc = tt
        chalo = lambda col: (lambda i, j: (jnp.maximum(i * (tc // SUBLANES) - 1, 0), col + j))
        bc, utail = _conv(proj, b_col, c_col, x_col, proj, chalo(c_col), proj, chalo(x_col),
                          w["conv_w"], w["conv_b"], tc, seg_len // tc)
        new_k = proj.reshape(n_seg, seg_len, -1)[:, -WINDOW:, d_q:d_q + d_kv]
        new_v = proj.reshape(n_seg, seg_len, -1)[:, -WINDOW:, d_q + d_kv:d_q + 2 * d_kv]
    else:
        assert seg_len == CHUNK
        pk = prev_k.reshape(n_seg * WINDOW, d_kv)
        pv = prev_v.reshape(n_seg * WINDOW, d_kv)
        seg = lambda i: (i, 0)
        attn = _attention(w["attn_sinks"], proj, 0, proj, k_col, proj, v_col,
                          pk, seg, pv, seg, n, seg_len, 0)
        hist = jnp.pad(prev_conv, ((0, 0), (SUBLANES - (CONV_W - 1), 0), (0, 0))).reshape(n_seg * SUBLANES, d_conv)
        hmap = lambda i, j: (i, j)
        bc, utail = _conv(proj, b_col, c_col, x_col, hist, hmap, jnp.ones_like(hist), hmap,
                          w["conv_w"], w["conv_b"], seg_len, 0)
        k_new = proj[:, d_q:d_q + d_kv].reshape(n_seg, seg_len, d_kv)
        v_new = proj[:, d_q + d_kv:d_q + 2 * d_kv].reshape(n_seg, seg_len, d_kv)
        new_k = jnp.concatenate([prev_k.reshape(n_seg, WINDOW, d_kv), k_new], axis=1)[:, -WINDOW:]
        new_v = jnp.concatenate([prev_v.reshape(n_seg, WINDOW, d_kv), v_new], axis=1)[:, -WINDOW:]
    new_conv = utail.reshape(n_seg, -1, SUBLANES, d_conv)[:, -1, SUBLANES - (CONV_W - 1):]

    merged = _merge(attn, bc, w["w_proj_a"], w["w_proj_b"], proj, ga_col, gb_col)
    h, xn = _outproj(merged, x, w["w_out"], w["norm_ffn_g"])
    xt, s2, e2, th, e1 = _route(xn, w["peer_w_query"], w["peer_keys1"], w["peer_keys2"])
    pt = _peer(xt, s2, e2, th, e1, w["peer_u"], w["peer_vt"])
    y = _final(h, pt, norm_final_g, last)
    shape5 = (n_seg, WINDOW, N_KV_HEADS, HEAD_DIM)
    return y, new_k.reshape(shape5), new_v.reshape(shape5), new_conv


def kernel(x_prompt, x_sample, state_attn_k, state_attn_v, state_conv, norm_mix_g, w_in, attn_sinks, conv_w,
           conv_b, w_proj_a, w_proj_b, w_out, norm_ffn_g, peer_w_query, peer_keys1, peer_keys2, peer_u, peer_v,
           norm_final_g):
    depth = w_in.shape[0]
    bp, sp, d = x_prompt.shape
    bs, ss, _ = x_sample.shape
    yp = x_prompt.reshape(bp * sp, d)
    ys = x_sample.reshape(bs * ss, d)
    outs = [[] for _ in range(6)]
    for l in range(depth):
        w = dict(
            norm_mix_g=norm_mix_g[l], w_in=w_in[l].astype(BF16), attn_sinks=attn_sinks[l],
            conv_w=conv_w[l], conv_b=conv_b[l], w_proj_a=w_proj_a[l].astype(BF16),
            w_proj_b=w_proj_b[l].astype(BF16), w_out=w_out[l].astype(BF16), norm_ffn_g=norm_ffn_g[l],
            peer_w_query=peer_w_query[l].astype(BF16), peer_keys1=peer_keys1[l].astype(BF16),
            peer_keys2=peer_keys2[l].astype(BF16), peer_u=peer_u[l],
            peer_vt=_transpose(peer_v[l]),
        )
        last = l == depth - 1
        yp, k1, v1, c1 = _layer(yp, sp, None, None, None, w, norm_final_g, last)
        ys, k2, v2, c2 = _layer(ys, ss, state_attn_k[l], state_attn_v[l], state_conv[l], w, norm_final_g, last)
        for lst, val in zip(outs, (k1, v1, c1, k2, v2, c2)):
            lst.append(val)
    return (yp.reshape(bp, sp, d), ys.reshape(bs, ss, d)) + tuple(jnp.stack(o) for o in outs)
```

```python
import functools
import math

import jax
import jax.numpy as jnp
from jax import lax
from jax.experimental import pallas as pl
from jax.experimental.pallas import tpu as pltpu

F32 = jnp.float32
BF16 = jnp.bfloat16

CHUNK = 64
N_HEADS = 32
N_KV_HEADS = 4
GROUP = N_HEADS // N_KV_HEADS
HEAD_DIM = 64
WINDOW = 128
BAND = WINDOW + CHUNK
CONV_W = 3
PEER_HEADS = 8
N_KEYS = 128
D_HALF = 128
TOPK = 16
EPS = 1e-6
NEG_INF = -1e30

SUBLANES = 8
LANES = 128
VMEM_LIMIT = 60 * 1024 * 1024
COL_BLOCK = 512
EXPERT_BLOCK = 1024

_NT = (((1,), (1,)), ((), ()))


def _params(*sem):
    return pltpu.CompilerParams(dimension_semantics=sem, vmem_limit_bytes=VMEM_LIMIT)


def _tile(n, pref):
    t = min(n, pref)
    while n % t:
        t //= 2
    return t


def _oddeven_merge_sort(n):
    pairs = []

    def merge(lo, m, r):
        step = r * 2
        if step < m:
            merge(lo, m, step)
            merge(lo + r, m, step)
            for i in range(lo + r, lo + m - r, step):
                pairs.append((i, i + r))
        else:
            pairs.append((lo, lo + r))

    def sort(lo, m):
        if m > 1:
            half = m // 2
            sort(lo, half)
            sort(lo + half, half)
            merge(lo, m, 1)

    sort(0, n)
    return pairs


def _bitonic_merge(n):
    pairs = []
    d = n // 2
    while d >= 1:
        for i in range(n):
            if (i & d) == 0:
                pairs.append((i, i + d))
        d //= 2
    return pairs


_SORT16 = _oddeven_merge_sort(TOPK)
_BITONIC16 = _bitonic_merge(TOPK)


def _apply_net(pairs, xs):
    xs = list(xs)
    for i, j in pairs:
        a, b = xs[i], xs[j]
        xs[i] = jnp.maximum(a, b)
        xs[j] = jnp.minimum(a, b)
    return xs


def _top16_bitonic(a, b):
    return [jnp.maximum(a[i], b[TOPK - 1 - i]) for i in range(TOPK)]


def _inproj_kernel(x_ref, g_ref, w_ref, o_ref, xn_ref):
    @pl.when(pl.program_id(1) == 0)
    def _():
        x = x_ref[...]
        r = lax.rsqrt(jnp.mean(x * x, axis=-1, keepdims=True) + EPS)
        xn_ref[...] = (x * r * g_ref[...]).astype(BF16)

    o_ref[...] = jnp.dot(xn_ref[...], w_ref[...], preferred_element_type=F32)


def _inproj(x, g, w):
    n, d = x.shape
    d_in = w.shape[1]
    tm = _tile(n, 2048)
    tn = COL_BLOCK
    return pl.pallas_call(
        _inproj_kernel,
        grid=(n // tm, d_in // tn),
        in_specs=[
            pl.BlockSpec((tm, d), lambda i, j: (i, 0)),
            pl.BlockSpec((1, d), lambda i, j: (0, 0)),
            pl.BlockSpec((d, tn), lambda i, j: (0, j)),
        ],
        out_specs=pl.BlockSpec((tm, tn), lambda i, j: (i, j)),
        out_shape=jax.ShapeDtypeStruct((n, d_in), F32),
        scratch_shapes=[pltpu.VMEM((tm, d), BF16)],
        compiler_params=_params("parallel", "arbitrary"),
        name="inproj",
    )(x, g.reshape(1, d), w)


def _attn_kernel(sink_ref, q_ref, k_ref, v_ref, pk_ref, pv_ref, o_ref, kall, vall, *,
                 n_chunks, tiles_per_seg):
    kall[0:WINDOW, :] = pk_ref[...].astype(BF16)
    vall[0:WINDOW, :] = pv_ref[...].astype(BF16)
    kall[WINDOW:, :] = k_ref[...].astype(BF16)
    vall[WINDOW:, :] = v_ref[...].astype(BF16)
    nq = GROUP * CHUNK
    if tiles_per_seg:
        band_chunk = lax.broadcasted_iota(jnp.int32, (BAND, nq), 0) // CHUNK
    q_group = lax.broadcasted_iota(jnp.int32, (1, nq), 1) // CHUNK
    sink_rows = []
    for kh in range(N_KV_HEADS):
        row = jnp.zeros((1, nq), F32)
        for g in range(GROUP):
            row = jnp.where(q_group == g, sink_ref[kh * GROUP + g], row)
        sink_rows.append(row)

    def chunk_pair(masked, i, carry):
        for u in range(unroll):
            chunk_body(masked, i * unroll + u)
        return carry

    def chunk_body(masked, c):
        r0 = pl.multiple_of(c * CHUNK, CHUNK)
        qc = q_ref[pl.ds(r0, CHUNK), :] * (HEAD_DIM ** -0.5)
        kb = kall[pl.ds(r0, BAND), :]
        vb = vall[pl.ds(r0, BAND), :]
        if masked:
            valid = (c + band_chunk) >= (WINDOW // CHUNK)
        outs = []
        for kh in range(N_KV_HEADS):
            k_h = kb[:, kh * HEAD_DIM:(kh + 1) * HEAD_DIM]
            v_h = vb[:, kh * HEAD_DIM:(kh + 1) * HEAD_DIM]
            q_h = jnp.concatenate(
                [qc[:, (kh * GROUP + g) * HEAD_DIM:(kh * GROUP + g + 1) * HEAD_DIM] for g in range(GROUP)],
                axis=0).astype(BF16)
            s = lax.dot_general(k_h, q_h, _NT, preferred_element_type=F32)
            if masked:
                s = jnp.where(valid, s, NEG_INF)
            sink = sink_rows[kh]
            m = jnp.maximum(jnp.max(s, axis=0, keepdims=True), sink)
            p = jnp.exp(s - m)
            probs = p / (jnp.sum(p, axis=0, keepdims=True) + jnp.exp(sink - m))
            o_t = lax.dot_general(v_h, probs.astype(BF16), (((0,), (0,)), ((), ())),
                                  preferred_element_type=F32)
            o = o_t.T
            outs += [o[g * CHUNK:(g + 1) * CHUNK, :] for g in range(GROUP)]
        o_ref[pl.ds(r0, CHUNK), :] = jnp.concatenate(outs, axis=1).astype(BF16)

    unroll = 4 if n_chunks % 4 == 0 else 1
    if tiles_per_seg:
        first = pl.program_id(0) % tiles_per_seg == 0

        @pl.when(first)
        def _():
            lax.fori_loop(0, n_chunks // unroll, functools.partial(chunk_pair, True), 0)

        @pl.when(jnp.logical_not(first))
        def _():
            lax.fori_loop(0, n_chunks // unroll, functools.partial(chunk_pair, False), 0)
    else:
        lax.fori_loop(0, n_chunks // unroll, functools.partial(chunk_pair, False), 0)


def _attention(sinks, q_arr, q_col, k_arr, k_col, v_arr, v_col, pk_arr, pk_map, pv_arr, pv_map,
               n, tt, tiles_per_seg):
    d_q = N_HEADS * HEAD_DIM
    d_kv = N_KV_HEADS * HEAD_DIM
    body = functools.partial(_attn_kernel, n_chunks=tt // CHUNK, tiles_per_seg=tiles_per_seg)
    return pl.pallas_call(
        body,
        grid=(n // tt,),
        in_specs=[
            pl.BlockSpec(memory_space=pltpu.SMEM),
            pl.BlockSpec((tt, d_q), lambda i: (i, q_col)),
            pl.BlockSpec((tt, d_kv), lambda i: (i, k_col)),
            pl.BlockSpec((tt, d_kv), lambda i: (i, v_col)),
            pl.BlockSpec((WINDOW, d_kv), pk_map),
            pl.BlockSpec((WINDOW, d_kv), pv_map),
        ],
        out_specs=pl.BlockSpec((tt, d_q), lambda i: (i, 0)),
        out_shape=jax.ShapeDtypeStruct((n, d_q), BF16),
        scratch_shapes=[pltpu.VMEM((WINDOW + tt, d_kv), BF16), pltpu.VMEM((WINDOW + tt, d_kv), BF16)],
        compiler_params=_params("arbitrary"),
        name="attention",
    )(sinks, q_arr, k_arr, v_arr, pk_arr, pv_arr)


def _conv_kernel(b_ref, c_ref, x_ref, hc_ref, hx_ref, w_ref, cb_ref, bc_ref, ut_ref, *, tiles_per_seg):
    u = c_ref[...] * x_ref[...]
    uh = hc_ref[...] * hx_ref[...]
    if tiles_per_seg:
        uh = jnp.where(pl.program_id(0) % tiles_per_seg == 0, 0.0, uh)
    row = lax.broadcasted_iota(jnp.int32, u.shape, 0)
    um1 = jnp.where(row == 0, uh[7:8, :], pltpu.roll(u, 1, axis=0))
    um2 = jnp.where(row == 0, uh[6:7, :], jnp.where(row == 1, uh[7:8, :], pltpu.roll(u, 2, axis=0)))
    conv = cb_ref[...] + w_ref[0:1, :] * um2
    conv = conv + w_ref[1:2, :] * um1
    conv = conv + w_ref[2:3, :] * u
    bc_ref[...] = (b_ref[...] * conv).astype(BF16)
    ut_ref[...] = u[u.shape[0] - SUBLANES:, :]


def _conv(proj, b_col, c_col, x_col, hc_arr, hc_map, hx_arr, hx_map, conv_w, conv_b, tm, tiles_per_seg):
    n = proj.shape[0]
    d_conv = conv_w.shape[1]
    tn = COL_BLOCK
    nh = d_conv // tn
    body = functools.partial(_conv_kernel, tiles_per_seg=tiles_per_seg)
    return pl.pallas_call(
        body,
        grid=(n // tm, nh),
        in_specs=[
            pl.BlockSpec((tm, tn), lambda i, j: (i, b_col + j)),
            pl.BlockSpec((tm, tn), lambda i, j: (i, c_col + j)),
            pl.BlockSpec((tm, tn), lambda i, j: (i, x_col + j)),
            pl.BlockSpec((SUBLANES, tn), hc_map),
            pl.BlockSpec((SUBLANES, tn), hx_map),
            pl.BlockSpec((CONV_W, tn), lambda i, j: (0, j)),
            pl.BlockSpec((1, tn), lambda i, j: (0, j)),
        ],
        out_specs=[
            pl.BlockSpec((tm, tn), lambda i, j: (i, j)),
            pl.BlockSpec((SUBLANES, tn), lambda i, j: (i, j)),
        ],
        out_shape=[
            jax.ShapeDtypeStruct((n, d_conv), BF16),
            jax.ShapeDtypeStruct((n // tm * SUBLANES, d_conv), F32),
        ],
        compiler_params=_params("arbitrary", "arbitrary"),
        name="conv",
    )(proj, proj, proj, hc_arr, hx_arr, conv_w, conv_b.reshape(1, d_conv))


def _merge_kernel(a_ref, bc_ref, wa_ref, wb_ref, ga_ref, gb_ref, o_ref):
    ya = jnp.dot(a_ref[...], wa_ref[...], preferred_element_type=F32)
    yb = jnp.dot(bc_ref[...], wb_ref[...], preferred_element_type=F32)
    o_ref[...] = (jax.nn.sigmoid(ga_ref[...]) * ya + jax.nn.sigmoid(gb_ref[...]) * yb).astype(BF16)


def _merge(attn, bc, wa, wb, proj, ga_col, gb_col):
    n, d_q = attn.shape
    d_conv = bc.shape[1]
    d = wa.shape[1]
    tm = _tile(n, 1024)
    tn = COL_BLOCK
    return pl.pallas_call(
        _merge_kernel,
        grid=(n // tm, d // tn),
        in_specs=[
            pl.BlockSpec((tm, d_q), lambda i, j: (i, 0)),
            pl.BlockSpec((tm, d_conv), lambda i, j: (i, 0)),
            pl.BlockSpec((d_q, tn), lambda i, j: (0, j)),
            pl.BlockSpec((d_conv, tn), lambda i, j: (0, j)),
            pl.BlockSpec((tm, tn), lambda i, j: (i, ga_col + j)),
            pl.BlockSpec((tm, tn), lambda i, j: (i, gb_col + j)),
        ],
        out_specs=pl.BlockSpec((tm, tn), lambda i, j: (i, j)),
        out_shape=jax.ShapeDtypeStruct((n, d), BF16),
        compiler_params=_params("parallel", "arbitrary"),
        name="merge",
    )(attn, bc, wa, wb, proj, proj)


def _outproj_kernel(m_ref, x_ref, w_ref, g_ref, h_ref, xn_ref):
    h = x_ref[...] + jnp.dot(m_ref[...], w_ref[...], preferred_element_type=F32)
    h_ref[...] = h
    r = lax.rsqrt(jnp.mean(h * h, axis=-1, keepdims=True) + EPS)
    xn_ref[...] = (h * r * g_ref[...]).astype(BF16)


def _outproj(merged, x, w, g):
    n, d = x.shape
    tm = _tile(n, 512)
    return pl.pallas_call(
        _outproj_kernel,
        grid=(n // tm,),
        in_specs=[
            pl.BlockSpec((tm, d), lambda i: (i, 0)),
            pl.BlockSpec((tm, d), lambda i: (i, 0)),
            pl.BlockSpec((d, d), lambda i: (0, 0)),
            pl.BlockSpec((1, d), lambda i: (0, 0)),
        ],
        out_specs=[pl.BlockSpec((tm, d), lambda i: (i, 0)), pl.BlockSpec((tm, d), lambda i: (i, 0))],
        out_shape=[jax.ShapeDtypeStruct((n, d), F32), jax.ShapeDtypeStruct((n, d), BF16)],
        compiler_params=_params("parallel"),
        name="outproj",
    )(merged, x, w, g.reshape(1, d))


def _pair_threshold(v1, v2):
    c = lambda i, j: v1[i] + v2[j]
    row0 = [c(0, j) for j in range(TOPK)]
    grp1 = [c(1, j) for j in range(8)] + [c(2, j) for j in range(5)] + [c(3, j) for j in range(3)]
    grp2 = ([c(3, 3)] + [c(4, j) for j in range(3)] + [c(5, 0), c(5, 1), c(6, 0), c(6, 1), c(7, 0), c(7, 1)]
            + [c(i, 0) for i in range(8, 14)])
    top = _apply_net(_BITONIC16, _top16_bitonic(row0, _apply_net(_SORT16, grp1)))
    top = _apply_net(_BITONIC16, _top16_bitonic(top, _apply_net(_SORT16, grp2)))
    top[TOPK - 1] = jnp.maximum(top[TOPK - 1], c(14, 0))
    top[TOPK - 2] = jnp.maximum(top[TOPK - 2], c(15, 0))
    tau = functools.reduce(jnp.minimum, top)
    m = row0[0]
    z = functools.reduce(lambda a, b: a + b, [jnp.exp(t - m) for t in top])
    return tau, z


TOP_R = 3
GEN_J = TOPK // (TOP_R + 1)


def _route_kernel(xn_ref, wq_ref, k1_ref, k2_ref, xt_ref, key2_ref, key1_ref, q_scr, s_scr, top_scr):
    tb = xn_ref.shape[0]
    q_scr[...] = jnp.dot(xn_ref[...], wq_ref[...], preferred_element_type=F32).astype(BF16)
    xt_ref[...] = xn_ref[...].astype(F32).T.astype(BF16)
    for h in range(PEER_HEADS):
        rows = slice(h * N_KEYS, (h + 1) * N_KEYS)
        q1 = q_scr[:, 2 * h * D_HALF:(2 * h + 1) * D_HALF]
        q2 = q_scr[:, (2 * h + 1) * D_HALF:(2 * h + 2) * D_HALF]
        s_scr[0, rows, :] = lax.dot_general(k1_ref[h], q1, _NT, preferred_element_type=F32)
        s_scr[1, rows, :] = lax.dot_general(k2_ref[h], q2, _NT, preferred_element_type=F32)

    def lane_block(lb, carry):
        cols = pl.ds(pl.multiple_of(lb * LANES, LANES), LANES)
        for side in range(2):
            for h in range(PEER_HEADS):
                lst = [s_scr[side, pl.ds(h * N_KEYS + SUBLANES * r, SUBLANES), cols]
                       for r in range(N_KEYS // SUBLANES)]
                lst = _apply_net(_SORT16, lst)
                for d in (4, 2, 1):
                    other = [pltpu.roll(x, d, axis=0) for x in lst]
                    lst = _apply_net(_BITONIC16, _top16_bitonic(lst, other))
                for i in range(TOPK):
                    top_scr[side, pl.ds(i * PEER_HEADS + h, 1), cols] = lst[i][0:1, :]
        v1 = [top_scr[0, pl.ds(i * PEER_HEADS, PEER_HEADS), cols] for i in range(TOPK)]
        v2 = [top_scr[1, pl.ds(i * PEER_HEADS, PEER_HEADS), cols] for i in range(TOPK)]
        tau, z = _pair_threshold(v1, v2)
        reach = []
        for j in range(GEN_J):
            t = jnp.full_like(tau, jnp.inf)
            for r in range(TOPK // (j + 1)):
                t = jnp.where(v1[r] + v2[j] >= tau, v1[r], t)
            reach.append(t)
        floor_top = []
        for r in range(TOP_R):
            t = jnp.full_like(tau, jnp.inf)
            for j in range(TOPK // (r + 1)):
                t = jnp.where(v1[r] + v2[j] >= tau, v2[j], t)
            floor_top.append(t)
        for h in range(PEER_HEADS):
            rows = slice(h * N_KEYS, (h + 1) * N_KEYS)
            s1 = s_scr[0, rows, cols]
            s2 = s_scr[1, rows, cols]
            th = jnp.full_like(s1, jnp.inf)
            for j in range(GEN_J):
                th = jnp.where(s1 >= reach[j][h:h + 1, :], v2[j][h:h + 1, :], th)
            for r in reversed(range(TOP_R)):
                th = jnp.where(s1 >= v1[r][h:h + 1, :], floor_top[r][h:h + 1, :], th)
            key1_ref[0, rows, cols] = th
            key1_ref[1, rows, cols] = jnp.exp(s1 - v1[0][h:h + 1, :])
            key2_ref[0, rows, cols] = s2
            key2_ref[1, rows, cols] = jnp.exp(s2 - v2[0][h:h + 1, :]) / z[h:h + 1, :]
        return carry

    lax.fori_loop(0, tb // LANES, lane_block, 0)


def _route(xn, wq, k1, k2):
    n, d = xn.shape
    dq = wq.shape[1]
    tb = _tile(n, 512)
    rows = PEER_HEADS * N_KEYS
    tok = lambda i: (0, i)
    return pl.pallas_call(
        _route_kernel,
        grid=(n // tb,),
        in_specs=[
            pl.BlockSpec((tb, d), lambda i: (i, 0)),
            pl.BlockSpec((d, dq), lambda i: (0, 0)),
            pl.BlockSpec((PEER_HEADS, N_KEYS, D_HALF), lambda i: (0, 0, 0)),
            pl.BlockSpec((PEER_HEADS, N_KEYS, D_HALF), lambda i: (0, 0, 0)),
        ],
        out_specs=[pl.BlockSpec((d, tb), tok)] + [pl.BlockSpec((2, rows, tb), lambda i: (0, 0, i))] * 2,
        out_shape=[jax.ShapeDtypeStruct((d, n), BF16)] + [jax.ShapeDtypeStruct((2, rows, n), F32)] * 2,
        scratch_shapes=[
            pltpu.VMEM((tb, dq), BF16),
            pltpu.VMEM((2, rows, tb), F32),
            pltpu.VMEM((2, TOPK * PEER_HEADS, tb), F32),
        ],
        compiler_params=_params("parallel"),
        name="peer_route",
    )(xn, wq, k1, k2)


def _peer_kernel(xt_ref, key2_ref, key1_ref, u_ref, vt_ref, o_ref, h0, h1, a0, a1, *, nblk):
    eb, tb = h0.shape
    d = o_ref.shape[0]
    n_i1 = eb // N_KEYS
    rb = 2 * SUBLANES
    n_lb = tb // LANES
    hrows = eb // n_lb
    orows = d // n_lb
    s = pl.program_id(0)

    @pl.when(s == 0)
    def _():
        for ref in (h0, h1, a0, a1):
            ref[...] = jnp.zeros_like(ref)

    @pl.when((s < 2) | ((s - 2) % nblk == 0))
    def _():
        o_ref[...] = jnp.zeros_like(o_ref)

    def step_part(h_new, h_prev, a_prev, a_old, lb):
        hr = pl.ds(lb * hrows, hrows)
        h_new[hr, :] = jnp.dot(u_ref[hr, :], xt_ref[...], preferred_element_type=F32)

        cols = pl.ds(lb * LANES, LANES)
        for r0 in range(0, N_KEYS, rb):
            gate = [None] * n_i1
            for h in range(PEER_HEADS):
                s2 = key2_ref[0, h, r0:r0 + rb, cols]
                e2 = key2_ref[1, h, r0:r0 + rb, cols]
                for a in range(n_i1):
                    t = jnp.where(s2 >= key1_ref[0, h, a:a + 1, cols], e2, 0.0) * key1_ref[1, h, a:a + 1, cols]
                    gate[a] = t if gate[a] is None else gate[a] + t
            for a in range(n_i1):
                hid = h_prev[a * N_KEYS + r0:a * N_KEYS + r0 + rb, cols]
                act = 0.5 * hid * (1.0 + lax.erf(hid * (1.0 / math.sqrt(2.0))))
                a_prev[a * N_KEYS + r0:a * N_KEYS + r0 + rb, cols] = (act * gate[a]).astype(BF16)

        orow = pl.ds(lb * orows, orows)
        o_ref[orow, :] += jnp.dot(vt_ref[orow, :], a_old[...], preferred_element_type=F32)

    @pl.when(s % 2 == 0)
    def _():
        for lb in range(n_lb):
            step_part(h0, h1, a1, a0, lb)

    @pl.when(s % 2 == 1)
    def _():
        for lb in range(n_lb):
            step_part(h1, h0, a0, a1, lb)


def _peer(xt, key2, key1, u, vt):
    d, n = xt.shape
    n_exp = u.shape[0]
    tb = _tile(n, 512)
    eb = EXPERT_BLOCK
    n_i1 = eb // N_KEYS
    nblk = n_exp // eb
    n_tiles = n // tb
    last = n_tiles * nblk - 1
    r4 = lambda a: a.reshape(2, PEER_HEADS, N_KEYS, n)
    step = lambda s, lag: jnp.clip(s - lag, 0, last)
    tile = lambda s, lag: step(s, lag) // nblk
    blk = lambda s, lag: step(s, lag) % nblk
    return pl.pallas_call(
        functools.partial(_peer_kernel, nblk=nblk),
        grid=(n_tiles * nblk + 2,),
        in_specs=[
            pl.BlockSpec((d, tb), lambda s: (0, tile(s, 0))),
            pl.BlockSpec((2, PEER_HEADS, N_KEYS, tb), lambda s: (0, 0, 0, tile(s, 1))),
            pl.BlockSpec((2, PEER_HEADS, n_i1, tb), lambda s: (0, 0, blk(s, 1), tile(s, 1))),
            pl.BlockSpec((eb, d), lambda s: (blk(s, 0), 0)),
            pl.BlockSpec((d, eb), lambda s: (0, blk(s, 2))),
        ],
        out_specs=pl.BlockSpec((d, tb), lambda s: (0, tile(s, 2))),
        out_shape=jax.ShapeDtypeStruct((d, n), F32),
        scratch_shapes=[pltpu.VMEM((eb, tb), F32)] * 2 + [pltpu.VMEM((eb, tb), BF16)] * 2,
        compiler_params=_params("arbitrary"),
        name="peer_dense",
    )(xt, r4(key2), r4(key1), u, vt)


def _final_kernel(h_ref, pt_ref, g_ref, y_ref, *, normalize):
    y = h_ref[...] + pt_ref[...].T
    if normalize:
        r = lax.rsqrt(jnp.mean(y * y, axis=-1, keepdims=True) + EPS)
        y = y * r * g_ref[...]
    y_ref[...] = y


def _final(h, pt, g, normalize):
    n, d = h.shape
    tm = _tile(n, 512)
    return pl.pallas_call(
        functools.partial(_final_kernel, normalize=normalize),
        grid=(n // tm,),
        in_specs=[
            pl.BlockSpec((tm, d), lambda i: (i, 0)),
            pl.BlockSpec((d, tm), lambda i: (0, i)),
            pl.BlockSpec((1, d), lambda i: (0, 0)),
        ],
        out_specs=pl.BlockSpec((tm, d), lambda i: (i, 0)),
        out_shape=jax.ShapeDtypeStruct((n, d), F32),
        compiler_params=_params("parallel"),
        name="final",
    )(h, pt, g.reshape(1, d))


def _transpose_kernel(x_ref, o_ref):
    o_ref[...] = x_ref[...].T


def _transpose(x):
    r, c = x.shape
    tr = _tile(r, 512)
    return pl.pallas_call(
        _transpose_kernel,
        grid=(r // tr,),
        in_specs=[pl.BlockSpec((tr, c), lambda i: (i, 0))],
        out_specs=pl.BlockSpec((c, tr), lambda i: (0, i)),
        out_shape=jax.ShapeDtypeStruct((c, r), x.dtype),
        compiler_params=_params("parallel"),
        name="transpose",
    )(x)


def _layer(x, seg_len, prev_k, prev_v, prev_conv, w, norm_final_g, last):
    n, d = x.shape
    n_seg = n // seg_len
    d_q = N_HEADS * HEAD_DIM
    d_kv = N_KV_HEADS * HEAD_DIM
    d_conv = w["conv_w"].shape[1]
    proj = _inproj(x, w["norm_mix_g"], w["w_in"])
    k_col, v_col = d_q // d_kv, d_q // d_kv + 1
    o_b = d_q + 2 * d_kv
    cb = COL_BLOCK
    b_col, c_col, x_col = o_b // cb, (o_b + d_conv) // cb, (o_b + 2 * d_conv) // cb
    ga_col, gb_col = (o_b + 3 * d_conv) // cb, (o_b + 3 * d_conv + d) // cb
    assert o_b % cb == 0 and d_conv % cb == 0 and d % cb == 0 and d_q % d_kv == 0

    if prev_k is None:
        tt = _tile(seg_len, 512)
        assert tt % WINDOW == 0
        tps = seg_len // tt
        halo = lambda col: (lambda i: (jnp.maximum(i * (tt // WINDOW) - 1, 0), col))
        attn = _attention(w["attn_sinks"], proj, 0, proj, k_col, proj, v_col,
                          proj, halo(k_col), proj, halo(v_col), n, tt, tps)
        tc = tt
        chalo = lambda col: (lambda i, j: (jnp.maximum(i * (tc // SUBLANES) - 1, 0), col + j))
        bc, utail = _conv(proj, b_col, c_col, x_col, proj, chalo(c_col), proj, chalo(x_col),
                          w["conv_w"], w["conv_b"], tc, seg_len // tc)
        new_k = proj.reshape(n_seg, seg_len, -1)[:, -WINDOW:, d_q:d_q + d_kv]
        new_v = proj.reshape(n_seg, seg_len, -1)[:, -WINDOW:, d_q + d_kv:d_q + 2 * d_kv]
    else:
        assert seg_len == CHUNK
        pk = prev_k.reshape(n_seg * WINDOW, d_kv)
        pv = prev_v.reshape(n_seg * WINDOW, d_kv)
        seg = lambda i: (i, 0)
        attn = _attention(w["attn_sinks"], proj, 0, proj, k_col, proj, v_col,
                          pk, seg, pv, seg, n, seg_len, 0)
        hist = jnp.pad(prev_conv, ((0, 0), (SUBLANES - (CONV_W - 1), 0), (0, 0))).reshape(n_seg * SUBLANES, d_conv)
        hmap = lambda i, j: (i, j)
        bc, utail = _conv(proj, b_col, c_col, x_col, hist, hmap, jnp.ones_like(hist), hmap,
                          w["conv_w"], w["conv_b"], seg_len, 0)
        k_new = proj[:, d_q:d_q + d_kv].reshape(n_seg, seg_len, d_kv)
        v_new = proj[:, d_q + d_kv:d_q + 2 * d_kv].reshape(n_seg, seg_len, d_kv)
        new_k = jnp.concatenate([prev_k.reshape(n_seg, WINDOW, d_kv), k_new], axis=1)[:, -WINDOW:]
        new_v = jnp.concatenate([prev_v.reshape(n_seg, WINDOW, d_kv), v_new], axis=1)[:, -WINDOW:]
    new_conv = utail.reshape(n_seg, -1, SUBLANES, d_conv)[:, -1, SUBLANES - (CONV_W - 1):]

    merged = _merge(attn, bc, w["w_proj_a"], w["w_proj_b"], proj, ga_col, gb_col)
    h, xn = _outproj(merged, x, w["w_out"], w["norm_ffn_g"])
    xt, key2, key1 = _route(xn, w["peer_w_query"], w["peer_keys1"], w["peer_keys2"])
    pt = _peer(xt, key2, key1, w["peer_u"], w["peer_vt"])
    y = _final(h, pt, norm_final_g, last)
    shape5 = (n_seg, WINDOW, N_KV_HEADS, HEAD_DIM)
    return y, new_k.reshape(shape5), new_v.reshape(shape5), new_conv


def kernel(x_prompt, x_sample, state_attn_k, state_attn_v, state_conv, norm_mix_g, w_in, attn_sinks, conv_w,
           conv_b, w_proj_a, w_proj_b, w_out, norm_ffn_g, peer_w_query, peer_keys1, peer_keys2, peer_u, peer_v,
           norm_final_g):
    depth = w_in.shape[0]
    bp, sp, d = x_prompt.shape
    bs, ss, _ = x_sample.shape
    yp = x_prompt.reshape(bp * sp, d)
    ys = x_sample.reshape(bs * ss, d)
    outs = [[] for _ in range(6)]
    for l in range(depth):
        w = dict(
            norm_mix_g=norm_mix_g[l], w_in=w_in[l].astype(BF16), attn_sinks=attn_sinks[l],
            conv_w=conv_w[l], conv_b=conv_b[l], w_proj_a=w_proj_a[l].astype(BF16),
            w_proj_b=w_proj_b[l].astype(BF16), w_out=w_out[l].astype(BF16), norm_ffn_g=norm_ffn_g[l],
            peer_w_query=peer_w_query[l].astype(BF16), peer_keys1=peer_keys1[l].astype(BF16),
            peer_keys2=peer_keys2[l].astype(BF16), peer_u=peer_u[l],
            peer_vt=_transpose(peer_v[l]),
        )
        last = l == depth - 1
        yp, k1, v1, c1 = _layer(yp, sp, None, None, None, w, norm_final_g, last)
        ys, k2, v2, c2 = _layer(ys, ss, state_attn_k[l], state_attn_v[l], state_conv[l], w, norm_final_g, last)
        for lst, val in zip(outs, (k1, v1, c1, k2, v2, c2)):
            lst.append(val)
    return (yp.reshape(bp, sp, d), ys.reshape(bs, ss, d)) + tuple(jnp.stack(o) for o in outs)
```

```python
import functools
import math

import jax
import jax.numpy as jnp
from jax import lax
from jax.experimental import pallas as pl
from jax.experimental.pallas import tpu as pltpu

F32 = jnp.float32
BF16 = jnp.bfloat16

CHUNK = 64
N_HEADS = 32
N_KV_HEADS = 4
GROUP = N_HEADS // N_KV_HEADS
HEAD_DIM = 64
WINDOW = 128
BAND = WINDOW + CHUNK
CONV_W = 3
PEER_HEADS = 8
N_KEYS = 128
D_HALF = 128
TOPK = 16
EPS = 1e-6
NEG_INF = -1e30

SUBLANES = 8
LANES = 128
VMEM_LIMIT = 60 * 1024 * 1024
COL_BLOCK = 512
EXPERT_BLOCK = 1024
ROWS_STREAM = 2048
ROWS_WIDE = 1024
ROWS_RESIDENT = 512

_NT = (((1,), (1,)), ((), ()))


def _params(*sem):
    return pltpu.CompilerParams(dimension_semantics=sem, vmem_limit_bytes=VMEM_LIMIT)


def _tile(n, pref):
    t = min(n, pref)
    while n % t:
        t //= 2
    return t


def _oddeven_merge_sort(n):
    pairs = []

    def merge(lo, m, r):
        step = r * 2
        if step < m:
            merge(lo, m, step)
            merge(lo + r, m, step)
            for i in range(lo + r, lo + m - r, step):
                pairs.append((i, i + r))
        else:
            pairs.append((lo, lo + r))

    def sort(lo, m):
        if m > 1:
            half = m // 2
            sort(lo, half)
            sort(lo + half, half)
            merge(lo, m, 1)

    sort(0, n)
    return pairs


def _bitonic_merge(n):
    pairs = []
    d = n // 2
    while d >= 1:
        for i in range(n):
            if (i & d) == 0:
                pairs.append((i, i + d))
        d //= 2
    return pairs


_SORT16 = _oddeven_merge_sort(TOPK)
_BITONIC16 = _bitonic_merge(TOPK)


def _apply_net(pairs, xs):
    xs = list(xs)
    for i, j in pairs:
        a, b = xs[i], xs[j]
        xs[i] = jnp.maximum(a, b)
        xs[j] = jnp.minimum(a, b)
    return xs


def _top16_bitonic(a, b):
    return [jnp.maximum(a[i], b[TOPK - 1 - i]) for i in range(TOPK)]


def _inproj_kernel(x_ref, g_ref, w_ref, o_ref, xn_ref):
    @pl.when(pl.program_id(1) == 0)
    def _():
        x = x_ref[...]
        r = lax.rsqrt(jnp.mean(x * x, axis=-1, keepdims=True) + EPS)
        xn_ref[...] = (x * r * g_ref[...]).astype(BF16)

    o_ref[...] = jnp.dot(xn_ref[...], w_ref[...], preferred_element_type=F32)


def _inproj(x, g, w):
    n, d = x.shape
    d_in = w.shape[1]
    tm = _tile(n, ROWS_STREAM)
    tn = COL_BLOCK
    return pl.pallas_call(
        _inproj_kernel,
        grid=(n // tm, d_in // tn),
        in_specs=[
            pl.BlockSpec((tm, d), lambda i, j: (i, 0)),
            pl.BlockSpec((1, d), lambda i, j: (0, 0)),
            pl.BlockSpec((d, tn), lambda i, j: (0, j)),
        ],
        out_specs=pl.BlockSpec((tm, tn), lambda i, j: (i, j)),
        out_shape=jax.ShapeDtypeStruct((n, d_in), F32),
        scratch_shapes=[pltpu.VMEM((tm, d), BF16)],
        compiler_params=_params("parallel", "arbitrary"),
        name="inproj",
    )(x, g.reshape(1, d), w)


def _attn_kernel(sink_ref, q_ref, k_ref, v_ref, pk_ref, pv_ref, o_ref, kall, vall, *,
                 n_chunks, tiles_per_seg):
    kall[0:WINDOW, :] = pk_ref[...].astype(BF16)
    vall[0:WINDOW, :] = pv_ref[...].astype(BF16)
    kall[WINDOW:, :] = k_ref[...].astype(BF16)
    vall[WINDOW:, :] = v_ref[...].astype(BF16)
    nq = GROUP * CHUNK
    if tiles_per_seg:
        band_chunk = lax.broadcasted_iota(jnp.int32, (BAND, nq), 0) // CHUNK
    q_group = lax.broadcasted_iota(jnp.int32, (1, nq), 1) // CHUNK
    sink_rows = []
    for kh in range(N_KV_HEADS):
        row = jnp.zeros((1, nq), F32)
        for g in range(GROUP):
            row = jnp.where(q_group == g, sink_ref[kh * GROUP + g], row)
        sink_rows.append(row)

    def chunk_pair(masked, i, carry):
        for u in range(unroll):
            chunk_body(masked, i * unroll + u)
        return carry

    def chunk_body(masked, c):
        r0 = pl.multiple_of(c * CHUNK, CHUNK)
        qc = q_ref[pl.ds(r0, CHUNK), :] * (HEAD_DIM ** -0.5)
        kb = kall[pl.ds(r0, BAND), :]
        vb = vall[pl.ds(r0, BAND), :]
        if masked:
            valid = (c + band_chunk) >= (WINDOW // CHUNK)
        outs = []
        for kh in range(N_KV_HEADS):
            k_h = kb[:, kh * HEAD_DIM:(kh + 1) * HEAD_DIM]
            v_h = vb[:, kh * HEAD_DIM:(kh + 1) * HEAD_DIM]
            q_h = jnp.concatenate(
                [qc[:, (kh * GROUP + g) * HEAD_DIM:(kh * GROUP + g + 1) * HEAD_DIM] for g in range(GROUP)],
                axis=0).astype(BF16)
            s = lax.dot_general(k_h, q_h, _NT, preferred_element_type=F32)
            if masked:
                s = jnp.where(valid, s, NEG_INF)
            sink = sink_rows[kh]
            m = jnp.maximum(jnp.max(s, axis=0, keepdims=True), sink)
            p = jnp.exp(s - m)
            probs = p / (jnp.sum(p, axis=0, keepdims=True) + jnp.exp(sink - m))
            o_t = lax.dot_general(v_h, probs.astype(BF16), (((0,), (0,)), ((), ())),
                                  preferred_element_type=F32)
            o = o_t.T
            outs += [o[g * CHUNK:(g + 1) * CHUNK, :] for g in range(GROUP)]
        o_ref[pl.ds(r0, CHUNK), :] = jnp.concatenate(outs, axis=1).astype(BF16)

    unroll = 4 if n_chunks % 4 == 0 else 1
    if tiles_per_seg:
        first = pl.program_id(0) % tiles_per_seg == 0

        @pl.when(first)
        def _():
            lax.fori_loop(0, n_chunks // unroll, functools.partial(chunk_pair, True), 0)

        @pl.when(jnp.logical_not(first))
        def _():
            lax.fori_loop(0, n_chunks // unroll, functools.partial(chunk_pair, False), 0)
    else:
        lax.fori_loop(0, n_chunks // unroll, functools.partial(chunk_pair, False), 0)


def _attention(sinks, q_arr, q_col, k_arr, k_col, v_arr, v_col, pk_arr, pk_map, pv_arr, pv_map,
               n, tt, tiles_per_seg):
    d_q = N_HEADS * HEAD_DIM
    d_kv = N_KV_HEADS * HEAD_DIM
    body = functools.partial(_attn_kernel, n_chunks=tt // CHUNK, tiles_per_seg=tiles_per_seg)
    return pl.pallas_call(
        body,
        grid=(n // tt,),
        in_specs=[
            pl.BlockSpec(memory_space=pltpu.SMEM),
            pl.BlockSpec((tt, d_q), lambda i: (i, q_col)),
            pl.BlockSpec((tt, d_kv), lambda i: (i, k_col)),
            pl.BlockSpec((tt, d_kv), lambda i: (i, v_col)),
            pl.BlockSpec((WINDOW, d_kv), pk_map),
            pl.BlockSpec((WINDOW, d_kv), pv_map),
        ],
        out_specs=pl.BlockSpec((tt, d_q), lambda i: (i, 0)),
        out_shape=jax.ShapeDtypeStruct((n, d_q), BF16),
        scratch_shapes=[pltpu.VMEM((WINDOW + tt, d_kv), BF16), pltpu.VMEM((WINDOW + tt, d_kv), BF16)],
        compiler_params=_params("arbitrary"),
        name="attention",
    )(sinks, q_arr, k_arr, v_arr, pk_arr, pv_arr)


def _conv_kernel(b_ref, c_ref, x_ref, hc_ref, hx_ref, w_ref, cb_ref, bc_ref, ut_ref, *, tiles_per_seg):
    u = c_ref[...] * x_ref[...]
    uh = hc_ref[...] * hx_ref[...]
    if tiles_per_seg:
        uh = jnp.where(pl.program_id(0) % tiles_per_seg == 0, 0.0, uh)
    row = lax.broadcasted_iota(jnp.int32, u.shape, 0)
    um1 = jnp.where(row == 0, uh[7:8, :], pltpu.roll(u, 1, axis=0))
    um2 = jnp.where(row == 0, uh[6:7, :], jnp.where(row == 1, uh[7:8, :], pltpu.roll(u, 2, axis=0)))
    conv = cb_ref[...] + w_ref[0:1, :] * um2
    conv = conv + w_ref[1:2, :] * um1
    conv = conv + w_ref[2:3, :] * u
    bc_ref[...] = (b_ref[...] * conv).astype(BF16)
    ut_ref[...] = u[u.shape[0] - SUBLANES:, :]


def _conv(proj, b_col, c_col, x_col, hc_arr, hc_map, hx_arr, hx_map, conv_w, conv_b, tm, tiles_per_seg):
    n = proj.shape[0]
    d_conv = conv_w.shape[1]
    tn = COL_BLOCK
    nh = d_conv // tn
    body = functools.partial(_conv_kernel, tiles_per_seg=tiles_per_seg)
    return pl.pallas_call(
        body,
        grid=(n // tm, nh),
        in_specs=[
            pl.BlockSpec((tm, tn), lambda i, j: (i, b_col + j)),
            pl.BlockSpec((tm, tn), lambda i, j: (i, c_col + j)),
            pl.BlockSpec((tm, tn), lambda i, j: (i, x_col + j)),
            pl.BlockSpec((SUBLANES, tn), hc_map),
            pl.BlockSpec((SUBLANES, tn), hx_map),
            pl.BlockSpec((CONV_W, tn), lambda i, j: (0, j)),
            pl.BlockSpec((1, tn), lambda i, j: (0, j)),
        ],
        out_specs=[
            pl.BlockSpec((tm, tn), lambda i, j: (i, j)),
            pl.BlockSpec((SUBLANES, tn), lambda i, j: (i, j)),
        ],
        out_shape=[
            jax.ShapeDtypeStruct((n, d_conv), BF16),
            jax.ShapeDtypeStruct((n // tm * SUBLANES, d_conv), F32),
        ],
        compiler_params=_params("arbitrary", "arbitrary"),
        name="conv",
    )(proj, proj, proj, hc_arr, hx_arr, conv_w, conv_b.reshape(1, d_conv))


def _merge_kernel(a_ref, bc_ref, wa_ref, wb_ref, ga_ref, gb_ref, o_ref):
    ya = jnp.dot(a_ref[...], wa_ref[...], preferred_element_type=F32)
    yb = jnp.dot(bc_ref[...], wb_ref[...], preferred_element_type=F32)
    o_ref[...] = (jax.nn.sigmoid(ga_ref[...]) * ya + jax.nn.sigmoid(gb_ref[...]) * yb).astype(BF16)


def _merge(attn, bc, wa, wb, proj, ga_col, gb_col):
    n, d_q = attn.shape
    d_conv = bc.shape[1]
    d = wa.shape[1]
    tm = _tile(n, ROWS_WIDE)
    tn = COL_BLOCK
    return pl.pallas_call(
        _merge_kernel,
        grid=(n // tm, d // tn),
        in_specs=[
            pl.BlockSpec((tm, d_q), lambda i, j: (i, 0)),
            pl.BlockSpec((tm, d_conv), lambda i, j: (i, 0)),
            pl.BlockSpec((d_q, tn), lambda i, j: (0, j)),
            pl.BlockSpec((d_conv, tn), lambda i, j: (0, j)),
            pl.BlockSpec((tm, tn), lambda i, j: (i, ga_col + j)),
            pl.BlockSpec((tm, tn), lambda i, j: (i, gb_col + j)),
        ],
        out_specs=pl.BlockSpec((tm, tn), lambda i, j: (i, j)),
        out_shape=jax.ShapeDtypeStruct((n, d), BF16),
        compiler_params=_params("parallel", "arbitrary"),
        name="merge",
    )(attn, bc, wa, wb, proj, proj)


def _outproj_kernel(m_ref, x_ref, w_ref, g_ref, h_ref, xn_ref):
    h = x_ref[...] + jnp.dot(m_ref[...], w_ref[...], preferred_element_type=F32)
    h_ref[...] = h
    r = lax.rsqrt(jnp.mean(h * h, axis=-1, keepdims=True) + EPS)
    xn_ref[...] = (h * r * g_ref[...]).astype(BF16)


def _outproj(merged, x, w, g):
    n, d = x.shape
    tm = _tile(n, ROWS_RESIDENT)
    return pl.pallas_call(
        _outproj_kernel,
        grid=(n // tm,),
        in_specs=[
            pl.BlockSpec((tm, d), lambda i: (i, 0)),
            pl.BlockSpec((tm, d), lambda i: (i, 0)),
            pl.BlockSpec((d, d), lambda i: (0, 0)),
            pl.BlockSpec((1, d), lambda i: (0, 0)),
        ],
        out_specs=[pl.BlockSpec((tm, d), lambda i: (i, 0)), pl.BlockSpec((tm, d), lambda i: (i, 0))],
        out_shape=[jax.ShapeDtypeStruct((n, d), F32), jax.ShapeDtypeStruct((n, d), BF16)],
        compiler_params=_params("parallel"),
        name="outproj",
    )(merged, x, w, g.reshape(1, d))


def _pair_threshold(v1, v2):
    c = lambda i, j: v1[i] + v2[j]
    row0 = [c(0, j) for j in range(TOPK)]
    grp1 = [c(1, j) for j in range(8)] + [c(2, j) for j in range(5)] + [c(3, j) for j in range(3)]
    grp2 = ([c(3, 3)] + [c(4, j) for j in range(3)] + [c(5, 0), c(5, 1), c(6, 0), c(6, 1), c(7, 0), c(7, 1)]
            + [c(i, 0) for i in range(8, 14)])
    top = _apply_net(_BITONIC16, _top16_bitonic(row0, _apply_net(_SORT16, grp1)))
    top = _apply_net(_BITONIC16, _top16_bitonic(top, _apply_net(_SORT16, grp2)))
    top[TOPK - 1] = jnp.maximum(top[TOPK - 1], c(14, 0))
    top[TOPK - 2] = jnp.maximum(top[TOPK - 2], c(15, 0))
    tau = functools.reduce(jnp.minimum, top)
    m = row0[0]
    z = functools.reduce(lambda a, b: a + b, [jnp.exp(t - m) for t in top])
    return tau, z


TOP_R = 3
GEN_J = TOPK // (TOP_R + 1)


def _route_kernel(xn_ref, wq_ref, k1_ref, k2_ref, xt_ref, key2_ref, key1_ref, q_scr, s_scr, top_scr):
    tb = xn_ref.shape[0]
    q_scr[...] = jnp.dot(xn_ref[...], wq_ref[...], preferred_element_type=F32).astype(BF16)
    xt_ref[...] = xn_ref[...].astype(F32).T.astype(BF16)
    for h in range(PEER_HEADS):
        rows = slice(h * N_KEYS, (h + 1) * N_KEYS)
        q1 = q_scr[:, 2 * h * D_HALF:(2 * h + 1) * D_HALF]
        q2 = q_scr[:, (2 * h + 1) * D_HALF:(2 * h + 2) * D_HALF]
        s_scr[0, rows, :] = lax.dot_general(k1_ref[h], q1, _NT, preferred_element_type=F32)
        s_scr[1, rows, :] = lax.dot_general(k2_ref[h], q2, _NT, preferred_element_type=F32)

    def lane_block(lb, carry):
        cols = pl.ds(pl.multiple_of(lb * LANES, LANES), LANES)
        for side in range(2):
            for h in range(PEER_HEADS):
                lst = [s_scr[side, pl.ds(h * N_KEYS + SUBLANES * r, SUBLANES), cols]
                       for r in range(N_KEYS // SUBLANES)]
                lst = _apply_net(_SORT16, lst)
                for d in (4, 2, 1):
                    other = [pltpu.roll(x, d, axis=0) for x in lst]
                    lst = _apply_net(_BITONIC16, _top16_bitonic(lst, other))
                for i in range(TOPK):
                    top_scr[side, pl.ds(i * PEER_HEADS + h, 1), cols] = lst[i][0:1, :]
        v1 = [top_scr[0, pl.ds(i * PEER_HEADS, PEER_HEADS), cols] for i in range(TOPK)]
        v2 = [top_scr[1, pl.ds(i * PEER_HEADS, PEER_HEADS), cols] for i in range(TOPK)]
        tau, z = _pair_threshold(v1, v2)
        reach = []
        for j in range(GEN_J):
            t = jnp.full_like(tau, jnp.inf)
            for r in range(TOPK // (j + 1)):
                t = jnp.where(v1[r] + v2[j] >= tau, v1[r], t)
            reach.append(t)
        floor_top = []
        for r in range(TOP_R):
            t = jnp.full_like(tau, jnp.inf)
            for j in range(TOPK // (r + 1)):
                t = jnp.where(v1[r] + v2[j] >= tau, v2[j], t)
            floor_top.append(t)
        for h in range(PEER_HEADS):
            rows = slice(h * N_KEYS, (h + 1) * N_KEYS)
            s1 = s_scr[0, rows, cols]
            s2 = s_scr[1, rows, cols]
            th = jnp.full_like(s1, jnp.inf)
            for j in range(GEN_J):
                th = jnp.where(s1 >= reach[j][h:h + 1, :], v2[j][h:h + 1, :], th)
            for r in reversed(range(TOP_R)):
                th = jnp.where(s1 >= v1[r][h:h + 1, :], floor_top[r][h:h + 1, :], th)
            key1_ref[0, rows, cols] = th
            key1_ref[1, rows, cols] = jnp.exp(s1 - v1[0][h:h + 1, :])
            key2_ref[0, rows, cols] = s2
            key2_ref[1, rows, cols] = jnp.exp(s2 - v2[0][h:h + 1, :]) / z[h:h + 1, :]
        return carry

    lax.fori_loop(0, tb // LANES, lane_block, 0)


def _route(xn, wq, k1, k2):
    n, d = xn.shape
    dq = wq.shape[1]
    tb = _tile(n, ROWS_RESIDENT)
    rows = PEER_HEADS * N_KEYS
    tok = lambda i: (0, i)
    return pl.pallas_call(
        _route_kernel,
        grid=(n // tb,),
        in_specs=[
            pl.BlockSpec((tb, d), lambda i: (i, 0)),
            pl.BlockSpec((d, dq), lambda i: (0, 0)),
            pl.BlockSpec((PEER_HEADS, N_KEYS, D_HALF), lambda i: (0, 0, 0)),
            pl.BlockSpec((PEER_HEADS, N_KEYS, D_HALF), lambda i: (0, 0, 0)),
        ],
        out_specs=[pl.BlockSpec((d, tb), tok)] + [pl.BlockSpec((2, rows, tb), lambda i: (0, 0, i))] * 2,
        out_shape=[jax.ShapeDtypeStruct((d, n), BF16)] + [jax.ShapeDtypeStruct((2, rows, n), F32)] * 2,
        scratch_shapes=[
            pltpu.VMEM((tb, dq), BF16),
            pltpu.VMEM((2, rows, tb), F32),
            pltpu.VMEM((2, TOPK * PEER_HEADS, tb), F32),
        ],
        compiler_params=_params("parallel"),
        name="peer_route",
    )(xn, wq, k1, k2)


def _peer_kernel(xt_ref, key2_ref, key1_ref, u_ref, vt_ref, o_ref, h0, h1, a0, a1, *, nblk):
    eb, tb = h0.shape
    d = o_ref.shape[0]
    n_i1 = eb // N_KEYS
    rb = 2 * SUBLANES
    n_lb = tb // LANES
    hrows = eb // n_lb
    orows = d // n_lb
    s = pl.program_id(0)

    @pl.when(s == 0)
    def _():
        for ref in (h0, h1, a0, a1):
            ref[...] = jnp.zeros_like(ref)

    @pl.when((s < 2) | ((s - 2) % nblk == 0))
    def _():
        o_ref[...] = jnp.zeros_like(o_ref)

    def step_part(h_new, h_prev, a_prev, a_old, lb):
        hr = pl.ds(lb * hrows, hrows)
        h_new[hr, :] = jnp.dot(u_ref[hr, :], xt_ref[...], preferred_element_type=F32)

        cols = pl.ds(lb * LANES, LANES)
        for r0 in range(0, N_KEYS, rb):
            gate = [None] * n_i1
            for h in range(PEER_HEADS):
                s2 = key2_ref[0, h, r0:r0 + rb, cols]
                e2 = key2_ref[1, h, r0:r0 + rb, cols]
                for a in range(n_i1):
                    t = jnp.where(s2 >= key1_ref[0, h, a:a + 1, cols], e2, 0.0) * key1_ref[1, h, a:a + 1, cols]
                    gate[a] = t if gate[a] is None else gate[a] + t
            for a in range(n_i1):
                hid = h_prev[a * N_KEYS + r0:a * N_KEYS + r0 + rb, cols]
                act = 0.5 * hid * (1.0 + lax.erf(hid * (1.0 / math.sqrt(2.0))))
                a_prev[a * N_KEYS + r0:a * N_KEYS + r0 + rb, cols] = (act * gate[a]).astype(BF16)

        orow = pl.ds(lb * orows, orows)
        o_ref[orow, :] += jnp.dot(vt_ref[orow, :], a_old[...], preferred_element_type=F32)

    @pl.when(s % 2 == 0)
    def _():
        for lb in range(n_lb):
            step_part(h0, h1, a1, a0, lb)

    @pl.when(s % 2 == 1)
    def _():
        for lb in range(n_lb):
            step_part(h1, h0, a0, a1, lb)


def _peer(xt, key2, key1, u, vt):
    d, n = xt.shape
    n_exp = u.shape[0]
    tb = _tile(n, ROWS_RESIDENT)
    eb = EXPERT_BLOCK
    n_i1 = eb // N_KEYS
    nblk = n_exp // eb
    n_tiles = n // tb
    last = n_tiles * nblk - 1
    r4 = lambda a: a.reshape(2, PEER_HEADS, N_KEYS, n)
    step = lambda s, lag: jnp.clip(s - lag, 0, last)
    tile = lambda s, lag: step(s, lag) // nblk
    blk = lambda s, lag: step(s, lag) % nblk
    return pl.pallas_call(
        functools.partial(_peer_kernel, nblk=nblk),
        grid=(n_tiles * nblk + 2,),
        in_specs=[
            pl.BlockSpec((d, tb), lambda s: (0, tile(s, 0))),
            pl.BlockSpec((2, PEER_HEADS, N_KEYS, tb), lambda s: (0, 0, 0, tile(s, 1))),
            pl.BlockSpec((2, PEER_HEADS, n_i1, tb), lambda s: (0, 0, blk(s, 1), tile(s, 1))),
            pl.BlockSpec((eb, d), lambda s: (blk(s, 0), 0)),
            pl.BlockSpec((d, eb), lambda s: (0, blk(s, 2))),
        ],
        out_specs=pl.BlockSpec((d, tb), lambda s: (0, tile(s, 2))),
        out_shape=jax.ShapeDtypeStruct((d, n), F32),
        scratch_shapes=[pltpu.VMEM((eb, tb), F32)] * 2 + [pltpu.VMEM((eb, tb), BF16)] * 2,
        compiler_params=_params("arbitrary"),
        name="peer_dense",
    )(xt, r4(key2), r4(key1), u, vt)


def _final_kernel(h_ref, pt_ref, g_ref, y_ref, *, normalize):
    y = h_ref[...] + pt_ref[...].T
    if normalize:
        r = lax.rsqrt(jnp.mean(y * y, axis=-1, keepdims=True) + EPS)
        y = y * r * g_ref[...]
    y_ref[...] = y


def _final(h, pt, g, normalize):
    n, d = h.shape
    tm = _tile(n, ROWS_WIDE)
    return pl.pallas_call(
        functools.partial(_final_kernel, normalize=normalize),
        grid=(n // tm,),
        in_specs=[
            pl.BlockSpec((tm, d), lambda i: (i, 0)),
            pl.BlockSpec((d, tm), lambda i: (0, i)),
            pl.BlockSpec((1, d), lambda i: (0, 0)),
        ],
        out_specs=pl.BlockSpec((tm, d), lambda i: (i, 0)),
        out_shape=jax.ShapeDtypeStruct((n, d), F32),
        compiler_params=_params("parallel"),
        name="final",
    )(h, pt, g.reshape(1, d))


def _transpose_kernel(x_ref, o_ref):
    o_ref[...] = x_ref[...].T


def _transpose(x):
    r, c = x.shape
    tr = _tile(r, ROWS_WIDE)
    return pl.pallas_call(
        _transpose_kernel,
        grid=(r // tr,),
        in_specs=[pl.BlockSpec((tr, c), lambda i: (i, 0))],
        out_specs=pl.BlockSpec((c, tr), lambda i: (0, i)),
        out_shape=jax.ShapeDtypeStruct((c, r), x.dtype),
        compiler_params=_params("parallel"),
        name="transpose",
    )(x)


def _layer(x, seg_len, prev_k, prev_v, prev_conv, w, norm_final_g, last):
    n, d = x.shape
    n_seg = n // seg_len
    d_q = N_HEADS * HEAD_DIM
    d_kv = N_KV_HEADS * HEAD_DIM
    d_conv = w["conv_w"].shape[1]
    proj = _inproj(x, w["norm_mix_g"], w["w_in"])
    k_col, v_col = d_q // d_kv, d_q // d_kv + 1
    o_b = d_q + 2 * d_kv
    cb = COL_BLOCK
    b_col, c_col, x_col = o_b // cb, (o_b + d_conv) // cb, (o_b + 2 * d_conv) // cb
    ga_col, gb_col = (o_b + 3 * d_conv) // cb, (o_b + 3 * d_conv + d) // cb
    assert o_b % cb == 0 and d_conv % cb == 0 and d % cb == 0 and d_q % d_kv == 0

    if prev_k is None:
        tt = _tile(seg_len, ROWS_WIDE)
        assert tt % WINDOW == 0
        tps = seg_len // tt
        halo = lambda col: (lambda i: (jnp.maximum(i * (tt // WINDOW) - 1, 0), col))
        attn = _attention(w["attn_sinks"], proj, 0, proj, k_col, proj, v_col,
                          proj, halo(k_col), proj, halo(v_col), n, tt, tps)
        tc = _tile(seg_len, ROWS_STREAM)
        chalo = lambda col: (lambda i, j: (jnp.maximum(i * (tc // SUBLANES) - 1, 0), col + j))
        bc, utail = _conv(proj, b_col, c_col, x_col, proj, chalo(c_col), proj, chalo(x_col),
                          w["conv_w"], w["conv_b"], tc, seg_len // tc)
        new_k = proj.reshape(n_seg, seg_len, -1)[:, -WINDOW:, d_q:d_q + d_kv]
        new_v = proj.reshape(n_seg, seg_len, -1)[:, -WINDOW:, d_q + d_kv:d_q + 2 * d_kv]
    else:
        assert seg_len == CHUNK
        pk = prev_k.reshape(n_seg * WINDOW, d_kv)
        pv = prev_v.reshape(n_seg * WINDOW, d_kv)
        seg = lambda i: (i, 0)
        attn = _attention(w["attn_sinks"], proj, 0, proj, k_col, proj, v_col,
                          pk, seg, pv, seg, n, seg_len, 0)
        hist = jnp.pad(prev_conv, ((0, 0), (SUBLANES - (CONV_W - 1), 0), (0, 0))).reshape(n_seg * SUBLANES, d_conv)
        hmap = lambda i, j: (i, j)
        bc, utail = _conv(proj, b_col, c_col, x_col, hist, hmap, jnp.ones_like(hist), hmap,
                          w["conv_w"], w["conv_b"], seg_len, 0)
        k_new = proj[:, d_q:d_q + d_kv].reshape(n_seg, seg_len, d_kv)
        v_new = proj[:, d_q + d_kv:d_q + 2 * d_kv].reshape(n_seg, seg_len, d_kv)
        new_k = jnp.concatenate([prev_k.reshape(n_seg, WINDOW, d_kv), k_new], axis=1)[:, -WINDOW:]
        new_v = jnp.concatenate([prev_v.reshape(n_seg, WINDOW, d_kv), v_new], axis=1)[:, -WINDOW:]
    new_conv = utail.reshape(n_seg, -1, SUBLANES, d_conv)[:, -1, SUBLANES - (CONV_W - 1):]

    merged = _merge(attn, bc, w["w_proj_a"], w["w_proj_b"], proj, ga_col, gb_col)
    h, xn = _outproj(merged, x, w["w_out"], w["norm_ffn_g"])
    xt, key2, key1 = _route(xn, w["peer_w_query"], w["peer_keys1"], w["peer_keys2"])
    pt = _peer(xt, key2, key1, w["peer_u"], w["peer_vt"])
    y = _final(h, pt, norm_final_g, last)
    shape5 = (n_seg, WINDOW, N_KV_HEADS, HEAD_DIM)
    return y, new_k.reshape(shape5), new_v.reshape(shape5), new_conv


def kernel(x_prompt, x_sample, state_attn_k, state_attn_v, state_conv, norm_mix_g, w_in, attn_sinks, conv_w,
           conv_b, w_proj_a, w_proj_b, w_out, norm_ffn_g, peer_w_query, peer_keys1, peer_keys2, peer_u, peer_v,
           norm_final_g):
    depth = w_in.shape[0]
    bp, sp, d = x_prompt.shape
    bs, ss, _ = x_sample.shape
    yp = x_prompt.reshape(bp * sp, d)
    ys = x_sample.reshape(bs * ss, d)
    outs = [[] for _ in range(6)]
    for l in range(depth):
        w = dict(
            norm_mix_g=norm_mix_g[l], w_in=w_in[l].astype(BF16), attn_sinks=attn_sinks[l],
            conv_w=conv_w[l], conv_b=conv_b[l], w_proj_a=w_proj_a[l].astype(BF16),
            w_proj_b=w_proj_b[l].astype(BF16), w_out=w_out[l].astype(BF16), norm_ffn_g=norm_ffn_g[l],
            peer_w_query=peer_w_query[l].astype(BF16), peer_keys1=peer_keys1[l].astype(BF16),
            peer_keys2=peer_keys2[l].astype(BF16), peer_u=peer_u[l],
            peer_vt=_transpose(peer_v[l]),
        )
        last = l == depth - 1
        yp, k1, v1, c1 = _layer(yp, sp, None, None, None, w, norm_final_g, last)
        ys, k2, v2, c2 = _layer(ys, ss, state_attn_k[l], state_attn_v[l], state_conv[l], w, norm_final_g, last)
        for lst, val in zip(outs, (k1, v1, c1, k2, v2, c2)):
            lst.append(val)
    return (yp.reshape(bp, sp, d), ys.reshape(bs, ss, d)) + tuple(jnp.stack(o) for o in outs)
```

```python
import functools
import math

import jax
import jax.numpy as jnp
from jax import lax
from jax.experimental import pallas as pl
from jax.experimental.pallas import tpu as pltpu

F32 = jnp.float32
BF16 = jnp.bfloat16

CHUNK = 64
N_HEADS = 32
N_KV_HEADS = 4
GROUP = N_HEADS // N_KV_HEADS
HEAD_DIM = 64
WINDOW = 128
BAND = WINDOW + CHUNK
CONV_W = 3
PEER_HEADS = 8
N_KEYS = 128
D_HALF = 128
TOPK = 16
EPS = 1e-6
NEG_INF = -1e30

SUBLANES = 8
LANES = 128
VMEM_LIMIT = 62 * 1024 * 1024
COL_BLOCK = 512
EXPERT_BLOCK = 1024
ROWS_STREAM = 2048
ROWS_WIDE = 1024
ROWS_RESIDENT = 512

_NT = (((1,), (1,)), ((), ()))


def _params(*sem):
    return pltpu.CompilerParams(dimension_semantics=sem, vmem_limit_bytes=VMEM_LIMIT)


def _tile(n, pref):
    t = min(n, pref)
    while n % t:
        t //= 2
    return t


def _oddeven_merge_sort(n):
    pairs = []

    def merge(lo, m, r):
        step = r * 2
        if step < m:
            merge(lo, m, step)
            merge(lo + r, m, step)
            for i in range(lo + r, lo + m - r, step):
                pairs.append((i, i + r))
        else:
            pairs.append((lo, lo + r))

    def sort(lo, m):
        if m > 1:
            half = m // 2
            sort(lo, half)
            sort(lo + half, half)
            merge(lo, m, 1)

    sort(0, n)
    return pairs


def _bitonic_merge(n):
    pairs = []
    d = n // 2
    while d >= 1:
        for i in range(n):
            if (i & d) == 0:
                pairs.append((i, i + d))
        d //= 2
    return pairs


_SORT16 = _oddeven_merge_sort(TOPK)
_BITONIC16 = _bitonic_merge(TOPK)


def _apply_net(pairs, xs):
    xs = list(xs)
    for i, j in pairs:
        a, b = xs[i], xs[j]
        xs[i] = jnp.maximum(a, b)
        xs[j] = jnp.minimum(a, b)
    return xs


def _top16_bitonic(a, b):
    return [jnp.maximum(a[i], b[TOPK - 1 - i]) for i in range(TOPK)]


def _inproj_kernel(x_ref, g_ref, w_ref, o_ref, xn_ref):
    @pl.when(pl.program_id(1) == 0)
    def _():
        x = x_ref[...]
        r = lax.rsqrt(jnp.mean(x * x, axis=-1, keepdims=True) + EPS)
        xn_ref[...] = (x * r * g_ref[...]).astype(BF16)

    o_ref[...] = jnp.dot(xn_ref[...], w_ref[...], preferred_element_type=F32)


def _inproj(x, g, w):
    n, d = x.shape
    d_in = w.shape[1]
    tm = _tile(n, ROWS_STREAM)
    tn = COL_BLOCK
    return pl.pallas_call(
        _inproj_kernel,
        grid=(n // tm, d_in // tn),
        in_specs=[
            pl.BlockSpec((tm, d), lambda i, j: (i, 0)),
            pl.BlockSpec((1, d), lambda i, j: (0, 0)),
            pl.BlockSpec((d, tn), lambda i, j: (0, j)),
        ],
        out_specs=pl.BlockSpec((tm, tn), lambda i, j: (i, j)),
        out_shape=jax.ShapeDtypeStruct((n, d_in), F32),
        scratch_shapes=[pltpu.VMEM((tm, d), BF16)],
        compiler_params=_params("parallel", "arbitrary"),
        name="inproj",
    )(x, g.reshape(1, d), w)


def _attn_kernel(sink_ref, q_ref, k_ref, v_ref, pk_ref, pv_ref, o_ref, kall, vall, *,
                 n_chunks, tiles_per_seg):
    kall[0:WINDOW, :] = pk_ref[...].astype(BF16)
    vall[0:WINDOW, :] = pv_ref[...].astype(BF16)
    kall[WINDOW:, :] = k_ref[...].astype(BF16)
    vall[WINDOW:, :] = v_ref[...].astype(BF16)
    nq = GROUP * CHUNK
    if tiles_per_seg:
        band_chunk = lax.broadcasted_iota(jnp.int32, (BAND, nq), 0) // CHUNK
    q_group = lax.broadcasted_iota(jnp.int32, (1, nq), 1) // CHUNK
    sink_rows = []
    for kh in range(N_KV_HEADS):
        row = jnp.zeros((1, nq), F32)
        for g in range(GROUP):
            row = jnp.where(q_group == g, sink_ref[kh * GROUP + g], row)
        sink_rows.append(row)

    def chunk_pair(masked, i, carry):
        for u in range(unroll):
            chunk_body(masked, i * unroll + u)
        return carry

    def chunk_body(masked, c):
        r0 = pl.multiple_of(c * CHUNK, CHUNK)
        qc = q_ref[pl.ds(r0, CHUNK), :] * (HEAD_DIM ** -0.5)
        kb = kall[pl.ds(r0, BAND), :]
        vb = vall[pl.ds(r0, BAND), :]
        if masked:
            valid = (c + band_chunk) >= (WINDOW // CHUNK)
        outs = []
        for kh in range(N_KV_HEADS):
            k_h = kb[:, kh * HEAD_DIM:(kh + 1) * HEAD_DIM]
            v_h = vb[:, kh * HEAD_DIM:(kh + 1) * HEAD_DIM]
            q_h = jnp.concatenate(
                [qc[:, (kh * GROUP + g) * HEAD_DIM:(kh * GROUP + g + 1) * HEAD_DIM] for g in range(GROUP)],
                axis=0).astype(BF16)
            s = lax.dot_general(k_h, q_h, _NT, preferred_element_type=F32)
            if masked:
                s = jnp.where(valid, s, NEG_INF)
            sink = sink_rows[kh]
            m = jnp.maximum(jnp.max(s, axis=0, keepdims=True), sink)
            p = jnp.exp(s - m)
            probs = p / (jnp.sum(p, axis=0, keepdims=True) + jnp.exp(sink - m))
            o_t = lax.dot_general(v_h, probs.astype(BF16), (((0,), (0,)), ((), ())),
                                  preferred_element_type=F32)
            o = o_t.T
            outs += [o[g * CHUNK:(g + 1) * CHUNK, :] for g in range(GROUP)]
        o_ref[pl.ds(r0, CHUNK), :] = jnp.concatenate(outs, axis=1).astype(BF16)

    unroll = 4 if n_chunks % 4 == 0 else 1
    if tiles_per_seg:
        first = pl.program_id(0) % tiles_per_seg == 0

        @pl.when(first)
        def _():
            lax.fori_loop(0, n_chunks // unroll, functools.partial(chunk_pair, True), 0)

        @pl.when(jnp.logical_not(first))
        def _():
            lax.fori_loop(0, n_chunks // unroll, functools.partial(chunk_pair, False), 0)
    else:
        lax.fori_loop(0, n_chunks // unroll, functools.partial(chunk_pair, False), 0)


def _attention(sinks, q_arr, q_col, k_arr, k_col, v_arr, v_col, pk_arr, pk_map, pv_arr, pv_map,
               n, tt, tiles_per_seg):
    d_q = N_HEADS * HEAD_DIM
    d_kv = N_KV_HEADS * HEAD_DIM
    body = functools.partial(_attn_kernel, n_chunks=tt // CHUNK, tiles_per_seg=tiles_per_seg)
    return pl.pallas_call(
        body,
        grid=(n // tt,),
        in_specs=[
            pl.BlockSpec(memory_space=pltpu.SMEM),
            pl.BlockSpec((tt, d_q), lambda i: (i, q_col)),
            pl.BlockSpec((tt, d_kv), lambda i: (i, k_col)),
            pl.BlockSpec((tt, d_kv), lambda i: (i, v_col)),
            pl.BlockSpec((WINDOW, d_kv), pk_map),
            pl.BlockSpec((WINDOW, d_kv), pv_map),
        ],
        out_specs=pl.BlockSpec((tt, d_q), lambda i: (i, 0)),
        out_shape=jax.ShapeDtypeStruct((n, d_q), BF16),
        scratch_shapes=[pltpu.VMEM((WINDOW + tt, d_kv), BF16), pltpu.VMEM((WINDOW + tt, d_kv), BF16)],
        compiler_params=_params("arbitrary"),
        name="attention",
    )(sinks, q_arr, k_arr, v_arr, pk_arr, pv_arr)


def _conv_kernel(b_ref, c_ref, x_ref, hc_ref, hx_ref, w_ref, cb_ref, bc_ref, ut_ref, *, tiles_per_seg):
    u = c_ref[...] * x_ref[...]
    uh = hc_ref[...] * hx_ref[...]
    if tiles_per_seg:
        uh = jnp.where(pl.program_id(0) % tiles_per_seg == 0, 0.0, uh)
    row = lax.broadcasted_iota(jnp.int32, u.shape, 0)
    um1 = jnp.where(row == 0, uh[7:8, :], pltpu.roll(u, 1, axis=0))
    um2 = jnp.where(row == 0, uh[6:7, :], jnp.where(row == 1, uh[7:8, :], pltpu.roll(u, 2, axis=0)))
    conv = cb_ref[...] + w_ref[0:1, :] * um2
    conv = conv + w_ref[1:2, :] * um1
    conv = conv + w_ref[2:3, :] * u
    bc_ref[...] = (b_ref[...] * conv).astype(BF16)
    ut_ref[...] = u[u.shape[0] - SUBLANES:, :]


def _conv(proj, b_col, c_col, x_col, hc_arr, hc_map, hx_arr, hx_map, conv_w, conv_b, tm, tiles_per_seg):
    n = proj.shape[0]
    d_conv = conv_w.shape[1]
    tn = COL_BLOCK
    nh = d_conv // tn
    body = functools.partial(_conv_kernel, tiles_per_seg=tiles_per_seg)
    return pl.pallas_call(
        body,
        grid=(n // tm, nh),
        in_specs=[
            pl.BlockSpec((tm, tn), lambda i, j: (i, b_col + j)),
            pl.BlockSpec((tm, tn), lambda i, j: (i, c_col + j)),
            pl.BlockSpec((tm, tn), lambda i, j: (i, x_col + j)),
            pl.BlockSpec((SUBLANES, tn), hc_map),
            pl.BlockSpec((SUBLANES, tn), hx_map),
            pl.BlockSpec((CONV_W, tn), lambda i, j: (0, j)),
            pl.BlockSpec((1, tn), lambda i, j: (0, j)),
        ],
        out_specs=[
            pl.BlockSpec((tm, tn), lambda i, j: (i, j)),
            pl.BlockSpec((SUBLANES, tn), lambda i, j: (i, j)),
        ],
        out_shape=[
            jax.ShapeDtypeStruct((n, d_conv), BF16),
            jax.ShapeDtypeStruct((n // tm * SUBLANES, d_conv), F32),
        ],
        compiler_params=_params("arbitrary", "arbitrary"),
        name="conv",
    )(proj, proj, proj, hc_arr, hx_arr, conv_w, conv_b.reshape(1, d_conv))


def _merge_kernel(a_ref, bc_ref, wa_ref, wb_ref, ga_ref, gb_ref, o_ref):
    ya = jnp.dot(a_ref[...], wa_ref[...], preferred_element_type=F32)
    yb = jnp.dot(bc_ref[...], wb_ref[...], preferred_element_type=F32)
    o_ref[...] = (jax.nn.sigmoid(ga_ref[...]) * ya + jax.nn.sigmoid(gb_ref[...]) * yb).astype(BF16)


def _merge(attn, bc, wa, wb, proj, ga_col, gb_col):
    n, d_q = attn.shape
    d_conv = bc.shape[1]
    d = wa.shape[1]
    tm = _tile(n, ROWS_WIDE)
    tn = COL_BLOCK
    return pl.pallas_call(
        _merge_kernel,
        grid=(n // tm, d // tn),
        in_specs=[
            pl.BlockSpec((tm, d_q), lambda i, j: (i, 0)),
            pl.BlockSpec((tm, d_conv), lambda i, j: (i, 0)),
            pl.BlockSpec((d_q, tn), lambda i, j: (0, j)),
            pl.BlockSpec((d_conv, tn), lambda i, j: (0, j)),
            pl.BlockSpec((tm, tn), lambda i, j: (i, ga_col + j)),
            pl.BlockSpec((tm, tn), lambda i, j: (i, gb_col + j)),
        ],
        out_specs=pl.BlockSpec((tm, tn), lambda i, j: (i, j)),
        out_shape=jax.ShapeDtypeStruct((n, d), BF16),
        compiler_params=_params("parallel", "arbitrary"),
        name="merge",
    )(attn, bc, wa, wb, proj, proj)


def _outproj_kernel(m_ref, x_ref, w_ref, g_ref, h_ref, xn_ref):
    h = x_ref[...] + jnp.dot(m_ref[...], w_ref[...], preferred_element_type=F32)
    h_ref[...] = h
    r = lax.rsqrt(jnp.mean(h * h, axis=-1, keepdims=True) + EPS)
    xn_ref[...] = (h * r * g_ref[...]).astype(BF16)


def _outproj(merged, x, w, g):
    n, d = x.shape
    tm = _tile(n, ROWS_RESIDENT)
    return pl.pallas_call(
        _outproj_kernel,
        grid=(n // tm,),
        in_specs=[
            pl.BlockSpec((tm, d), lambda i: (i, 0)),
            pl.BlockSpec((tm, d), lambda i: (i, 0)),
            pl.BlockSpec((d, d), lambda i: (0, 0)),
            pl.BlockSpec((1, d), lambda i: (0, 0)),
        ],
        out_specs=[pl.BlockSpec((tm, d), lambda i: (i, 0)), pl.BlockSpec((tm, d), lambda i: (i, 0))],
        out_shape=[jax.ShapeDtypeStruct((n, d), F32), jax.ShapeDtypeStruct((n, d), BF16)],
        compiler_params=_params("parallel"),
        name="outproj",
    )(merged, x, w, g.reshape(1, d))


def _pair_threshold(v1, v2):
    c = lambda i, j: v1[i] + v2[j]
    row0 = [c(0, j) for j in range(TOPK)]
    grp1 = [c(1, j) for j in range(8)] + [c(2, j) for j in range(5)] + [c(3, j) for j in range(3)]
    grp2 = ([c(3, 3)] + [c(4, j) for j in range(3)] + [c(5, 0), c(5, 1), c(6, 0), c(6, 1), c(7, 0), c(7, 1)]
            + [c(i, 0) for i in range(8, 14)])
    top = _apply_net(_BITONIC16, _top16_bitonic(row0, _apply_net(_SORT16, grp1)))
    top = _apply_net(_BITONIC16, _top16_bitonic(top, _apply_net(_SORT16, grp2)))
    top[TOPK - 1] = jnp.maximum(top[TOPK - 1], c(14, 0))
    top[TOPK - 2] = jnp.maximum(top[TOPK - 2], c(15, 0))
    tau = functools.reduce(jnp.minimum, top)
    m = row0[0]
    z = functools.reduce(lambda a, b: a + b, [jnp.exp(t - m) for t in top])
    return tau, z


TOP_R = 3
GEN_J = TOPK // (TOP_R + 1)


def _route_kernel(xn_ref, wq_ref, k1_ref, k2_ref, xt_ref, key2_ref, key1_ref, q_scr, s_scr, top_scr):
    tb = xn_ref.shape[0]
    q_scr[...] = jnp.dot(xn_ref[...], wq_ref[...], preferred_element_type=F32).astype(BF16)
    xt_ref[...] = xn_ref[...].astype(F32).T.astype(BF16)
    for h in range(PEER_HEADS):
        rows = slice(h * N_KEYS, (h + 1) * N_KEYS)
        q1 = q_scr[:, 2 * h * D_HALF:(2 * h + 1) * D_HALF]
        q2 = q_scr[:, (2 * h + 1) * D_HALF:(2 * h + 2) * D_HALF]
        s_scr[0, rows, :] = lax.dot_general(k1_ref[h], q1, _NT, preferred_element_type=F32)
        s_scr[1, rows, :] = lax.dot_general(k2_ref[h], q2, _NT, preferred_element_type=F32)

    def lane_block(lb, carry):
        cols = pl.ds(pl.multiple_of(lb * LANES, LANES), LANES)
        for side in range(2):
            for h in range(PEER_HEADS):
                lst = [s_scr[side, pl.ds(h * N_KEYS + SUBLANES * r, SUBLANES), cols]
                       for r in range(N_KEYS // SUBLANES)]
                lst = _apply_net(_SORT16, lst)
                for d in (4, 2, 1):
                    other = [pltpu.roll(x, d, axis=0) for x in lst]
                    lst = _apply_net(_BITONIC16, _top16_bitonic(lst, other))
                for i in range(TOPK):
                    top_scr[side, pl.ds(i * PEER_HEADS + h, 1), cols] = lst[i][0:1, :]
        v1 = [top_scr[0, pl.ds(i * PEER_HEADS, PEER_HEADS), cols] for i in range(TOPK)]
        v2 = [top_scr[1, pl.ds(i * PEER_HEADS, PEER_HEADS), cols] for i in range(TOPK)]
        tau, z = _pair_threshold(v1, v2)
        reach = []
        for j in range(GEN_J):
            t = jnp.full_like(tau, jnp.inf)
            for r in range(TOPK // (j + 1)):
                t = jnp.where(v1[r] + v2[j] >= tau, v1[r], t)
            reach.append(t)
        floor_top = []
        for r in range(TOP_R):
            t = jnp.full_like(tau, jnp.inf)
            for j in range(TOPK // (r + 1)):
                t = jnp.where(v1[r] + v2[j] >= tau, v2[j], t)
            floor_top.append(t)
        for h in range(PEER_HEADS):
            rows = slice(h * N_KEYS, (h + 1) * N_KEYS)
            s1 = s_scr[0, rows, cols]
            s2 = s_scr[1, rows, cols]
            th = jnp.full_like(s1, jnp.inf)
            for j in range(GEN_J):
                th = jnp.where(s1 >= reach[j][h:h + 1, :], v2[j][h:h + 1, :], th)
            for r in reversed(range(TOP_R)):
                th = jnp.where(s1 >= v1[r][h:h + 1, :], floor_top[r][h:h + 1, :], th)
            key1_ref[0, rows, cols] = th
            key1_ref[1, rows, cols] = jnp.exp(s1 - v1[0][h:h + 1, :])
            key2_ref[0, rows, cols] = s2
            key2_ref[1, rows, cols] = jnp.exp(s2 - v2[0][h:h + 1, :]) / z[h:h + 1, :]
        return carry

    lax.fori_loop(0, tb // LANES, lane_block, 0)


def _route(xn, wq, k1, k2):
    n, d = xn.shape
    dq = wq.shape[1]
    tb = _tile(n, ROWS_RESIDENT)
    rows = PEER_HEADS * N_KEYS
    tok = lambda i: (0, i)
    return pl.pallas_call(
        _route_kernel,
        grid=(n // tb,),
        in_specs=[
            pl.BlockSpec((tb, d), lambda i: (i, 0)),
            pl.BlockSpec((d, dq), lambda i: (0, 0)),
            pl.BlockSpec((PEER_HEADS, N_KEYS, D_HALF), lambda i: (0, 0, 0)),
            pl.BlockSpec((PEER_HEADS, N_KEYS, D_HALF), lambda i: (0, 0, 0)),
        ],
        out_specs=[pl.BlockSpec((d, tb), tok)] + [pl.BlockSpec((2, rows, tb), lambda i: (0, 0, i))] * 2,
        out_shape=[jax.ShapeDtypeStruct((d, n), BF16)] + [jax.ShapeDtypeStruct((2, rows, n), F32)] * 2,
        scratch_shapes=[
            pltpu.VMEM((tb, dq), BF16),
            pltpu.VMEM((2, rows, tb), F32),
            pltpu.VMEM((2, TOPK * PEER_HEADS, tb), F32),
        ],
        compiler_params=_params("parallel"),
        name="peer_route",
    )(xn, wq, k1, k2)


def _peer_kernel(xt_ref, key2_ref, key1_ref, u_hbm, vt_ref, o_ref, h0, h1, a0, a1, u_buf, u_sem, *, nblk, last):
    eb, tb = h0.shape
    d = o_ref.shape[0]
    n_i1 = eb // N_KEYS
    rb = 2 * SUBLANES
    n_lb = tb // LANES
    hrows = eb // n_lb
    orows = d // n_lb
    s = pl.program_id(0)
    depth = u_buf.shape[0]

    def u_copy(t):
        rows = pl.ds(pl.multiple_of((jnp.minimum(t, last) % nblk) * eb, eb), eb)
        slot = t % depth
        return pltpu.make_async_copy(u_hbm.at[rows, :], u_buf.at[slot], u_sem.at[slot])

    @pl.when(s == 0)
    def _():
        for ref in (h0, h1, a0, a1):
            ref[...] = jnp.zeros_like(ref)
        for t in range(depth - 1):
            u_copy(t).start()

    @pl.when(s + depth - 1 <= last + 2)
    def _():
        u_copy(s + depth - 1).start()

    u_copy(s).wait()
    u_ref = u_buf.at[s % depth]

    @pl.when((s < 2) | ((s - 2) % nblk == 0))
    def _():
        o_ref[...] = jnp.zeros_like(o_ref)

    def step_part(h_new, h_prev, a_prev, a_old, lb):
        hr = pl.ds(lb * hrows, hrows)
        h_new[hr, :] = jnp.dot(u_ref[hr, :], xt_ref[...], preferred_element_type=F32)

        cols = pl.ds(lb * LANES, LANES)
        for r0 in range(0, N_KEYS, rb):
            gate = [None] * n_i1
            for h in range(PEER_HEADS):
                s2 = key2_ref[0, h, r0:r0 + rb, cols]
                e2 = key2_ref[1, h, r0:r0 + rb, cols]
                for a in range(n_i1):
                    t = jnp.where(s2 >= key1_ref[0, h, a:a + 1, cols], e2, 0.0) * key1_ref[1, h, a:a + 1, cols]
                    gate[a] = t if gate[a] is None else gate[a] + t
            for a in range(n_i1):
                hid = h_prev[a * N_KEYS + r0:a * N_KEYS + r0 + rb, cols]
                act = 0.5 * hid * (1.0 + lax.erf(hid * (1.0 / math.sqrt(2.0))))
                a_prev[a * N_KEYS + r0:a * N_KEYS + r0 + rb, cols] = (act * gate[a]).astype(BF16)

        orow = pl.ds(lb * orows, orows)
        o_ref[orow, :] += jnp.dot(vt_ref[orow, :], a_old[...], preferred_element_type=F32)

    @pl.when(s % 2 == 0)
    def _():
        for lb in range(n_lb):
            step_part(h0, h1, a1, a0, lb)

    @pl.when(s % 2 == 1)
    def _():
        for lb in range(n_lb):
            step_part(h1, h0, a0, a1, lb)


def _peer(xt, key2, key1, u, vt):
    d, n = xt.shape
    n_exp = u.shape[0]
    tb = _tile(n, ROWS_RESIDENT)
    eb = EXPERT_BLOCK
    n_i1 = eb // N_KEYS
    nblk = n_exp // eb
    n_tiles = n // tb
    last = n_tiles * nblk - 1
    r4 = lambda a: a.reshape(2, PEER_HEADS, N_KEYS, n)
    step = lambda s, lag: jnp.clip(s - lag, 0, last)
    tile = lambda s, lag: step(s, lag) // nblk
    blk = lambda s, lag: step(s, lag) % nblk
    return pl.pallas_call(
        functools.partial(_peer_kernel, nblk=nblk, last=last),
        grid=(n_tiles * nblk + 2,),
        in_specs=[
            pl.BlockSpec((d, tb), lambda s: (0, tile(s, 0)), pipeline_mode=pl.Buffered(1)),
            pl.BlockSpec((2, PEER_HEADS, N_KEYS, tb), lambda s: (0, 0, 0, tile(s, 1)), pipeline_mode=pl.Buffered(1)),
            pl.BlockSpec((2, PEER_HEADS, n_i1, tb), lambda s: (0, 0, blk(s, 1), tile(s, 1))),
            pl.BlockSpec(memory_space=pl.ANY),
            pl.BlockSpec((d, eb), lambda s: (0, blk(s, 2))),
        ],
        out_specs=pl.BlockSpec((d, tb), lambda s: (0, tile(s, 2))),
        out_shape=jax.ShapeDtypeStruct((d, n), F32),
        scratch_shapes=([pltpu.VMEM((eb, tb), F32)] * 2 + [pltpu.VMEM((eb, tb), BF16)] * 2
                        + [pltpu.VMEM((3, eb, d), F32), pltpu.SemaphoreType.DMA((3,))]),
        compiler_params=_params("arbitrary"),
        name="peer_dense",
    )(xt, r4(key2), r4(key1), u, vt)


def _final_kernel(h_ref, pt_ref, g_ref, y_ref, *, normalize):
    y = h_ref[...] + pt_ref[...].T
    if normalize:
        r = lax.rsqrt(jnp.mean(y * y, axis=-1, keepdims=True) + EPS)
        y = y * r * g_ref[...]
    y_ref[...] = y


def _final(h, pt, g, normalize):
    n, d = h.shape
    tm = _tile(n, ROWS_WIDE)
    return pl.pallas_call(
        functools.partial(_final_kernel, normalize=normalize),
        grid=(n // tm,),
        in_specs=[
            pl.BlockSpec((tm, d), lambda i: (i, 0)),
            pl.BlockSpec((d, tm), lambda i: (0, i)),
            pl.BlockSpec((1, d), lambda i: (0, 0)),
        ],
        out_specs=pl.BlockSpec((tm, d), lambda i: (i, 0)),
        out_shape=jax.ShapeDtypeStruct((n, d), F32),
        compiler_params=_params("parallel"),
        name="final",
    )(h, pt, g.reshape(1, d))


def _transpose_kernel(x_ref, o_ref):
    o_ref[...] = x_ref[...].T


def _transpose(x):
    r, c = x.shape
    tr = _tile(r, ROWS_WIDE)
    return pl.pallas_call(
        _transpose_kernel,
        grid=(r // tr,),
        in_specs=[pl.BlockSpec((tr, c), lambda i: (i, 0))],
        out_specs=pl.BlockSpec((c, tr), lambda i: (0, i)),
        out_shape=jax.ShapeDtypeStruct((c, r), x.dtype),
        compiler_params=_params("parallel"),
        name="transpose",
    )(x)


def _layer(x, seg_len, prev_k, prev_v, prev_conv, w, norm_final_g, last):
    n, d = x.shape
    n_seg = n // seg_len
    d_q = N_HEADS * HEAD_DIM
    d_kv = N_KV_HEADS * HEAD_DIM
    d_conv = w["conv_w"].shape[1]
    proj = _inproj(x, w["norm_mix_g"], w["w_in"])
    k_col, v_col = d_q // d_kv, d_q // d_kv + 1
    o_b = d_q + 2 * d_kv
    cb = COL_BLOCK
    b_col, c_col, x_col = o_b // cb, (o_b + d_conv) // cb, (o_b + 2 * d_conv) // cb
    ga_col, gb_col = (o_b + 3 * d_conv) // cb, (o_b + 3 * d_conv + d) // cb
    assert o_b % cb == 0 and d_conv % cb == 0 and d % cb == 0 and d_q % d_kv == 0

    if prev_k is None:
        tt = _tile(seg_len, ROWS_WIDE)
        assert tt % WINDOW == 0
        tps = seg_len // tt
        halo = lambda col: (lambda i: (jnp.maximum(i * (tt // WINDOW) - 1, 0), col))
        attn = _attention(w["attn_sinks"], proj, 0, proj, k_col, proj, v_col,
                          proj, halo(k_col), proj, halo(v_col), n, tt, tps)
        tc = _tile(seg_len, ROWS_STREAM)
        chalo = lambda col: (lambda i, j: (jnp.maximum(i * (tc // SUBLANES) - 1, 0), col + j))
        bc, utail = _conv(proj, b_col, c_col, x_col, proj, chalo(c_col), proj, chalo(x_col),
                          w["conv_w"], w["conv_b"], tc, seg_len // tc)
        new_k = proj.reshape(n_seg, seg_len, -1)[:, -WINDOW:, d_q:d_q + d_kv]
        new_v = proj.reshape(n_seg, seg_len, -1)[:, -WINDOW:, d_q + d_kv:d_q + 2 * d_kv]
    else:
        assert seg_len == CHUNK
        pk = prev_k.reshape(n_seg * WINDOW, d_kv)
        pv = prev_v.reshape(n_seg * WINDOW, d_kv)
        seg = lambda i: (i, 0)
        attn = _attention(w["attn_sinks"], proj, 0, proj, k_col, proj, v_col,
                          pk, seg, pv, seg, n, seg_len, 0)
        hist = jnp.pad(prev_conv, ((0, 0), (SUBLANES - (CONV_W - 1), 0), (0, 0))).reshape(n_seg * SUBLANES, d_conv)
        hmap = lambda i, j: (i, j)
        bc, utail = _conv(proj, b_col, c_col, x_col, hist, hmap, jnp.ones_like(hist), hmap,
                          w["conv_w"], w["conv_b"], seg_len, 0)
        k_new = proj[:, d_q:d_q + d_kv].reshape(n_seg, seg_len, d_kv)
        v_new = proj[:, d_q + d_kv:d_q + 2 * d_kv].reshape(n_seg, seg_len, d_kv)
        new_k = jnp.concatenate([prev_k.reshape(n_seg, WINDOW, d_kv), k_new], axis=1)[:, -WINDOW:]
        new_v = jnp.concatenate([prev_v.reshape(n_seg, WINDOW, d_kv), v_new], axis=1)[:, -WINDOW:]
    new_conv = utail.reshape(n_seg, -1, SUBLANES, d_conv)[:, -1, SUBLANES - (CONV_W - 1):]

    merged = _merge(attn, bc, w["w_proj_a"], w["w_proj_b"], proj, ga_col, gb_col)
    h, xn = _outproj(merged, x, w["w_out"], w["norm_ffn_g"])
    xt, key2, key1 = _route(xn, w["peer_w_query"], w["peer_keys1"], w["peer_keys2"])
    pt = _peer(xt, key2, key1, w["peer_u"], w["peer_vt"])
    y = _final(h, pt, norm_final_g, last)
    shape5 = (n_seg, WINDOW, N_KV_HEADS, HEAD_DIM)
    return y, new_k.reshape(shape5), new_v.reshape(shape5), new_conv


def kernel(x_prompt, x_sample, state_attn_k, state_attn_v, state_conv, norm_mix_g, w_in, attn_sinks, conv_w,
           conv_b, w_proj_a, w_proj_b, w_out, norm_ffn_g, peer_w_query, peer_keys1, peer_keys2, peer_u, peer_v,
           norm_final_g):
    depth = w_in.shape[0]
    bp, sp, d = x_prompt.shape
    bs, ss, _ = x_sample.shape
    yp = x_prompt.reshape(bp * sp, d)
    ys = x_sample.reshape(bs * ss, d)
    outs = [[] for _ in range(6)]
    for l in range(depth):
        w = dict(
            norm_mix_g=norm_mix_g[l], w_in=w_in[l].astype(BF16), attn_sinks=attn_sinks[l],
            conv_w=conv_w[l], conv_b=conv_b[l], w_proj_a=w_proj_a[l].astype(BF16),
            w_proj_b=w_proj_b[l].astype(BF16), w_out=w_out[l].astype(BF16), norm_ffn_g=norm_ffn_g[l],
            peer_w_query=peer_w_query[l].astype(BF16), peer_keys1=peer_keys1[l].astype(BF16),
            peer_keys2=peer_keys2[l].astype(BF16), peer_u=peer_u[l],
            peer_vt=_transpose(peer_v[l]),
        )
        last = l == depth - 1
        yp, k1, v1, c1 = _layer(yp, sp, None, None, None, w, norm_final_g, last)
        ys, k2, v2, c2 = _layer(ys, ss, state_attn_k[l], state_attn_v[l], state_conv[l], w, norm_final_g, last)
        for lst, val in zip(outs, (k1, v1, c1, k2, v2, c2)):
            lst.append(val)
    return (yp.reshape(bp, sp, d), ys.reshape(bs, ss, d)) + tuple(jnp.stack(o) for o in outs)
```

```python
import functools
import math

import jax
import jax.numpy as jnp
from jax import lax
from jax.experimental import pallas as pl
from jax.experimental.pallas import tpu as pltpu

F32 = jnp.float32
BF16 = jnp.bfloat16

CHUNK = 64
N_HEADS = 32
N_KV_HEADS = 4
GROUP = N_HEADS // N_KV_HEADS
HEAD_DIM = 64
WINDOW = 128
BAND = WINDOW + CHUNK
CONV_W = 3
PEER_HEADS = 8
N_KEYS = 128
D_HALF = 128
TOPK = 16
EPS = 1e-6
NEG_INF = -1e30

SUBLANES = 8
LANES = 128
VMEM_LIMIT = 62 * 1024 * 1024
COL_BLOCK = 512
EXPERT_BLOCK = 1024
ROWS_STREAM = 2048
ROWS_WIDE = 1024
ROWS_RESIDENT = 512

_NT = (((1,), (1,)), ((), ()))


def _params(*sem):
    return pltpu.CompilerParams(dimension_semantics=sem, vmem_limit_bytes=VMEM_LIMIT)


def _tile(n, pref):
    t = min(n, pref)
    while n % t:
        t //= 2
    return t


def _oddeven_merge_sort(n):
    pairs = []

    def merge(lo, m, r):
        step = r * 2
        if step < m:
            merge(lo, m, step)
            merge(lo + r, m, step)
            for i in range(lo + r, lo + m - r, step):
                pairs.append((i, i + r))
        else:
            pairs.append((lo, lo + r))

    def sort(lo, m):
        if m > 1:
            half = m // 2
            sort(lo, half)
            sort(lo + half, half)
            merge(lo, m, 1)

    sort(0, n)
    return pairs


def _bitonic_merge(n):
    pairs = []
    d = n // 2
    while d >= 1:
        for i in range(n):
            if (i & d) == 0:
                pairs.append((i, i + d))
        d //= 2
    return pairs


_SORT16 = _oddeven_merge_sort(TOPK)
_BITONIC16 = _bitonic_merge(TOPK)


def _apply_net(pairs, xs):
    xs = list(xs)
    for i, j in pairs:
        a, b = xs[i], xs[j]
        xs[i] = jnp.maximum(a, b)
        xs[j] = jnp.minimum(a, b)
    return xs


def _top16_bitonic(a, b):
    return [jnp.maximum(a[i], b[TOPK - 1 - i]) for i in range(TOPK)]


def _inproj_kernel(x_ref, g_ref, w_hbm, o_ref, xn_ref, w_buf, w_sem, *, n_steps):
    n_col = pl.num_programs(1)
    tn = w_buf.shape[2]
    depth = w_buf.shape[0]
    t = pl.program_id(0) * n_col + pl.program_id(1)

    def w_copy(step):
        cols = pl.ds(pl.multiple_of((step % n_col) * tn, tn), tn)
        slot = step % depth
        return pltpu.make_async_copy(w_hbm.at[:, cols], w_buf.at[slot], w_sem.at[slot])

    @pl.when(t == 0)
    def _():
        for step in range(depth - 1):
            w_copy(step).start()

    @pl.when(t + depth - 1 < n_steps)
    def _():
        w_copy(t + depth - 1).start()

    @pl.when(pl.program_id(1) == 0)
    def _():
        x = x_ref[...]
        r = lax.rsqrt(jnp.mean(x * x, axis=-1, keepdims=True) + EPS)
        xn_ref[...] = (x * r * g_ref[...]).astype(BF16)

    w_copy(t).wait()
    o_ref[...] = jnp.dot(xn_ref[...], w_buf[t % depth], preferred_element_type=F32)


def _inproj(x, g, w):
    n, d = x.shape
    d_in = w.shape[1]
    tm = _tile(n, ROWS_STREAM)
    tn = COL_BLOCK
    n_steps = (n // tm) * (d_in // tn)
    depth = min(3, n_steps)
    return pl.pallas_call(
        functools.partial(_inproj_kernel, n_steps=n_steps),
        grid=(n // tm, d_in // tn),
        in_specs=[
            pl.BlockSpec((tm, d), lambda i, j: (i, 0)),
            pl.BlockSpec((1, d), lambda i, j: (0, 0)),
            pl.BlockSpec(memory_space=pl.ANY),
        ],
        out_specs=pl.BlockSpec((tm, tn), lambda i, j: (i, j)),
        out_shape=jax.ShapeDtypeStruct((n, d_in), F32),
        scratch_shapes=[pltpu.VMEM((tm, d), BF16), pltpu.VMEM((depth, d, tn), w.dtype),
                        pltpu.SemaphoreType.DMA((depth,))],
        compiler_params=_params("arbitrary", "arbitrary"),
        name="inproj",
    )(x, g.reshape(1, d), w)


def _attn_kernel(sink_ref, q_ref, k_ref, v_ref, pk_ref, pv_ref, o_ref, kall, vall, *,
                 n_chunks, tiles_per_seg):
    kall[0:WINDOW, :] = pk_ref[...].astype(BF16)
    vall[0:WINDOW, :] = pv_ref[...].astype(BF16)
    kall[WINDOW:, :] = k_ref[...].astype(BF16)
    vall[WINDOW:, :] = v_ref[...].astype(BF16)
    nq = GROUP * CHUNK
    if tiles_per_seg:
        band_chunk = lax.broadcasted_iota(jnp.int32, (BAND, nq), 0) // CHUNK
    q_group = lax.broadcasted_iota(jnp.int32, (1, nq), 1) // CHUNK
    sink_rows = []
    for kh in range(N_KV_HEADS):
        row = jnp.zeros((1, nq), F32)
        for g in range(GROUP):
            row = jnp.where(q_group == g, sink_ref[kh * GROUP + g], row)
        sink_rows.append(row)

    def chunk_pair(masked, i, carry):
        for u in range(unroll):
            chunk_body(masked, i * unroll + u)
        return carry

    def chunk_body(masked, c):
        r0 = pl.multiple_of(c * CHUNK, CHUNK)
        qc = q_ref[pl.ds(r0, CHUNK), :] * (HEAD_DIM ** -0.5)
        kb = kall[pl.ds(r0, BAND), :]
        vb = vall[pl.ds(r0, BAND), :]
        if masked:
            valid = (c + band_chunk) >= (WINDOW // CHUNK)
        outs = []
        for kh in range(N_KV_HEADS):
            k_h = kb[:, kh * HEAD_DIM:(kh + 1) * HEAD_DIM]
            v_h = vb[:, kh * HEAD_DIM:(kh + 1) * HEAD_DIM]
            q_h = jnp.concatenate(
                [qc[:, (kh * GROUP + g) * HEAD_DIM:(kh * GROUP + g + 1) * HEAD_DIM] for g in range(GROUP)],
                axis=0).astype(BF16)
            s = lax.dot_general(k_h, q_h, _NT, preferred_element_type=F32)
            if masked:
                s = jnp.where(valid, s, NEG_INF)
            sink = sink_rows[kh]
            m = jnp.maximum(jnp.max(s, axis=0, keepdims=True), sink)
            p = jnp.exp(s - m)
            probs = p / (jnp.sum(p, axis=0, keepdims=True) + jnp.exp(sink - m))
            o_t = lax.dot_general(v_h, probs.astype(BF16), (((0,), (0,)), ((), ())),
                                  preferred_element_type=F32)
            o = o_t.T
            outs += [o[g * CHUNK:(g + 1) * CHUNK, :] for g in range(GROUP)]
        o_ref[pl.ds(r0, CHUNK), :] = jnp.concatenate(outs, axis=1).astype(BF16)

    unroll = 4 if n_chunks % 4 == 0 else 1
    if tiles_per_seg:
        first = pl.program_id(0) % tiles_per_seg == 0

        @pl.when(first)
        def _():
            lax.fori_loop(0, n_chunks // unroll, functools.partial(chunk_pair, True), 0)

        @pl.when(jnp.logical_not(first))
        def _():
            lax.fori_loop(0, n_chunks // unroll, functools.partial(chunk_pair, False), 0)
    else:
        lax.fori_loop(0, n_chunks // unroll, functools.partial(chunk_pair, False), 0)


def _attention(sinks, q_arr, q_col, k_arr, k_col, v_arr, v_col, pk_arr, pk_map, pv_arr, pv_map,
               n, tt, tiles_per_seg):
    d_q = N_HEADS * HEAD_DIM
    d_kv = N_KV_HEADS * HEAD_DIM
    body = functools.partial(_attn_kernel, n_chunks=tt // CHUNK, tiles_per_seg=tiles_per_seg)
    return pl.pallas_call(
        body,
        grid=(n // tt,),
        in_specs=[
            pl.BlockSpec(memory_space=pltpu.SMEM),
            pl.BlockSpec((tt, d_q), lambda i: (i, q_col)),
            pl.BlockSpec((tt, d_kv), lambda i: (i, k_col)),
            pl.BlockSpec((tt, d_kv), lambda i: (i, v_col)),
            pl.BlockSpec((WINDOW, d_kv), pk_map),
            pl.BlockSpec((WINDOW, d_kv), pv_map),
        ],
        out_specs=pl.BlockSpec((tt, d_q), lambda i: (i, 0)),
        out_shape=jax.ShapeDtypeStruct((n, d_q), BF16),
        scratch_shapes=[pltpu.VMEM((WINDOW + tt, d_kv), BF16), pltpu.VMEM((WINDOW + tt, d_kv), BF16)],
        compiler_params=_params("arbitrary"),
        name="attention",
    )(sinks, q_arr, k_arr, v_arr, pk_arr, pv_arr)


def _conv_kernel(b_ref, c_ref, x_ref, hc_ref, hx_ref, w_ref, cb_ref, bc_ref, ut_ref, *, tiles_per_seg):
    u = c_ref[...] * x_ref[...]
    uh = hc_ref[...] * hx_ref[...]
    if tiles_per_seg:
        uh = jnp.where(pl.program_id(0) % tiles_per_seg == 0, 0.0, uh)
    row = lax.broadcasted_iota(jnp.int32, u.shape, 0)
    um1 = jnp.where(row == 0, uh[7:8, :], pltpu.roll(u, 1, axis=0))
    um2 = jnp.where(row == 0, uh[6:7, :], jnp.where(row == 1, uh[7:8, :], pltpu.roll(u, 2, axis=0)))
    conv = cb_ref[...] + w_ref[0:1, :] * um2
    conv = conv + w_ref[1:2, :] * um1
    conv = conv + w_ref[2:3, :] * u
    bc_ref[...] = (b_ref[...] * conv).astype(BF16)
    ut_ref[...] = u[u.shape[0] - SUBLANES:, :]


def _conv(proj, b_col, c_col, x_col, hc_arr, hc_map, hx_arr, hx_map, conv_w, conv_b, tm, tiles_per_seg):
    n = proj.shape[0]
    d_conv = conv_w.shape[1]
    tn = COL_BLOCK
    nh = d_conv // tn
    body = functools.partial(_conv_kernel, tiles_per_seg=tiles_per_seg)
    return pl.pallas_call(
        body,
        grid=(n // tm, nh),
        in_specs=[
            pl.BlockSpec((tm, tn), lambda i, j: (i, b_col + j)),
            pl.BlockSpec((tm, tn), lambda i, j: (i, c_col + j)),
            pl.BlockSpec((tm, tn), lambda i, j: (i, x_col + j)),
            pl.BlockSpec((SUBLANES, tn), hc_map),
            pl.BlockSpec((SUBLANES, tn), hx_map),
            pl.BlockSpec((CONV_W, tn), lambda i, j: (0, j)),
            pl.BlockSpec((1, tn), lambda i, j: (0, j)),
        ],
        out_specs=[
            pl.BlockSpec((tm, tn), lambda i, j: (i, j)),
            pl.BlockSpec((SUBLANES, tn), lambda i, j: (i, j)),
        ],
        out_shape=[
            jax.ShapeDtypeStruct((n, d_conv), BF16),
            jax.ShapeDtypeStruct((n // tm * SUBLANES, d_conv), F32),
        ],
        compiler_params=_params("arbitrary", "arbitrary"),
        name="conv",
    )(proj, proj, proj, hc_arr, hx_arr, conv_w, conv_b.reshape(1, d_conv))


def _merge_kernel(a_ref, bc_ref, wa_ref, wb_ref, ga_ref, gb_ref, o_ref):
    ya = jnp.dot(a_ref[...], wa_ref[...], preferred_element_type=F32)
    yb = jnp.dot(bc_ref[...], wb_ref[...], preferred_element_type=F32)
    o_ref[...] = (jax.nn.sigmoid(ga_ref[...]) * ya + jax.nn.sigmoid(gb_ref[...]) * yb).astype(BF16)


def _merge(attn, bc, wa, wb, proj, ga_col, gb_col):
    n, d_q = attn.shape
    d_conv = bc.shape[1]
    d = wa.shape[1]
    tm = _tile(n, ROWS_WIDE)
    tn = COL_BLOCK
    return pl.pallas_call(
        _merge_kernel,
        grid=(n // tm, d // tn),
        in_specs=[
            pl.BlockSpec((tm, d_q), lambda i, j: (i, 0)),
            pl.BlockSpec((tm, d_conv), lambda i, j: (i, 0)),
            pl.BlockSpec((d_q, tn), lambda i, j: (0, j)),
            pl.BlockSpec((d_conv, tn), lambda i, j: (0, j)),
            pl.BlockSpec((tm, tn), lambda i, j: (i, ga_col + j)),
            pl.BlockSpec((tm, tn), lambda i, j: (i, gb_col + j)),
        ],
        out_specs=pl.BlockSpec((tm, tn), lambda i, j: (i, j)),
        out_shape=jax.ShapeDtypeStruct((n, d), BF16),
        compiler_params=_params("parallel", "arbitrary"),
        name="merge",
    )(attn, bc, wa, wb, proj, proj)


def _outproj_kernel(m_ref, x_ref, w_ref, g_ref, h_ref, xn_ref):
    h = x_ref[...] + jnp.dot(m_ref[...], w_ref[...], preferred_element_type=F32)
    h_ref[...] = h
    r = lax.rsqrt(jnp.mean(h * h, axis=-1, keepdims=True) + EPS)
    xn_ref[...] = (h * r * g_ref[...]).astype(BF16)


def _outproj(merged, x, w, g):
    n, d = x.shape
    tm = _tile(n, ROWS_RESIDENT)
    return pl.pallas_call(
        _outproj_kernel,
        grid=(n // tm,),
        in_specs=[
            pl.BlockSpec((tm, d), lambda i: (i, 0)),
            pl.BlockSpec((tm, d), lambda i: (i, 0)),
            pl.BlockSpec((d, d), lambda i: (0, 0)),
            pl.BlockSpec((1, d), lambda i: (0, 0)),
        ],
        out_specs=[pl.BlockSpec((tm, d), lambda i: (i, 0)), pl.BlockSpec((tm, d), lambda i: (i, 0))],
        out_shape=[jax.ShapeDtypeStruct((n, d), F32), jax.ShapeDtypeStruct((n, d), BF16)],
        compiler_params=_params("parallel"),
        name="outproj",
    )(merged, x, w, g.reshape(1, d))


def _pair_threshold(v1, v2):
    c = lambda i, j: v1[i] + v2[j]
    row0 = [c(0, j) for j in range(TOPK)]
    grp1 = [c(1, j) for j in range(8)] + [c(2, j) for j in range(5)] + [c(3, j) for j in range(3)]
    grp2 = ([c(3, 3)] + [c(4, j) for j in range(3)] + [c(5, 0), c(5, 1), c(6, 0), c(6, 1), c(7, 0), c(7, 1)]
            + [c(i, 0) for i in range(8, 14)])
    top = _apply_net(_BITONIC16, _top16_bitonic(row0, _apply_net(_SORT16, grp1)))
    top = _apply_net(_BITONIC16, _top16_bitonic(top, _apply_net(_SORT16, grp2)))
    top[TOPK - 1] = jnp.maximum(top[TOPK - 1], c(14, 0))
    top[TOPK - 2] = jnp.maximum(top[TOPK - 2], c(15, 0))
    tau = functools.reduce(jnp.minimum, top)
    m = row0[0]
    z = functools.reduce(lambda a, b: a + b, [jnp.exp(t - m) for t in top])
    return tau, z


TOP_R = 3
GEN_J = TOPK // (TOP_R + 1)


def _route_kernel(xn_ref, wq_ref, k1_ref, k2_ref, xt_ref, key2_ref, key1_ref, q_scr, s_scr, top_scr):
    tb = xn_ref.shape[0]
    q_scr[...] = jnp.dot(xn_ref[...], wq_ref[...], preferred_element_type=F32).astype(BF16)
    xt_ref[...] = xn_ref[...].astype(F32).T.astype(BF16)
    for h in range(PEER_HEADS):
        rows = slice(h * N_KEYS, (h + 1) * N_KEYS)
        q1 = q_scr[:, 2 * h * D_HALF:(2 * h + 1) * D_HALF]
        q2 = q_scr[:, (2 * h + 1) * D_HALF:(2 * h + 2) * D_HALF]
        s_scr[0, rows, :] = lax.dot_general(k1_ref[h], q1, _NT, preferred_element_type=F32)
        s_scr[1, rows, :] = lax.dot_general(k2_ref[h], q2, _NT, preferred_element_type=F32)

    def lane_block(lb, carry):
        cols = pl.ds(pl.multiple_of(lb * LANES, LANES), LANES)
        for side in range(2):
            for h in range(PEER_HEADS):
                lst = [s_scr[side, pl.ds(h * N_KEYS + SUBLANES * r, SUBLANES), cols]
                       for r in range(N_KEYS // SUBLANES)]
                lst = _apply_net(_SORT16, lst)
                for d in (4, 2, 1):
                    other = [pltpu.roll(x, d, axis=0) for x in lst]
                    lst = _apply_net(_BITONIC16, _top16_bitonic(lst, other))
                for i in range(TOPK):
                    top_scr[side, pl.ds(i * PEER_HEADS + h, 1), cols] = lst[i][0:1, :]
        v1 = [top_scr[0, pl.ds(i * PEER_HEADS, PEER_HEADS), cols] for i in range(TOPK)]
        v2 = [top_scr[1, pl.ds(i * PEER_HEADS, PEER_HEADS), cols] for i in range(TOPK)]
        tau, z = _pair_threshold(v1, v2)
        reach = []
        for j in range(GEN_J):
            t = jnp.full_like(tau, jnp.inf)
            for r in range(TOPK // (j + 1)):
                t = jnp.where(v1[r] + v2[j] >= tau, v1[r], t)
            reach.append(t)
        floor_top = []
        for r in range(TOP_R):
            t = jnp.full_like(tau, jnp.inf)
            for j in range(TOPK // (r + 1)):
                t = jnp.where(v1[r] + v2[j] >= tau, v2[j], t)
            floor_top.append(t)
        for h in range(PEER_HEADS):
            rows = slice(h * N_KEYS, (h + 1) * N_KEYS)
            s1 = s_scr[0, rows, cols]
            s2 = s_scr[1, rows, cols]
            th = jnp.full_like(s1, jnp.inf)
            for j in range(GEN_J):
                th = jnp.where(s1 >= reach[j][h:h + 1, :], v2[j][h:h + 1, :], th)
            for r in reversed(range(TOP_R)):
                th = jnp.where(s1 >= v1[r][h:h + 1, :], floor_top[r][h:h + 1, :], th)
            key1_ref[0, rows, cols] = th
            key1_ref[1, rows, cols] = jnp.exp(s1 - v1[0][h:h + 1, :])
            key2_ref[0, rows, cols] = s2
            key2_ref[1, rows, cols] = jnp.exp(s2 - v2[0][h:h + 1, :]) / z[h:h + 1, :]
        return carry

    lax.fori_loop(0, tb // LANES, lane_block, 0)


def _route(xn, wq, k1, k2):
    n, d = xn.shape
    dq = wq.shape[1]
    tb = _tile(n, ROWS_RESIDENT)
    rows = PEER_HEADS * N_KEYS
    tok = lambda i: (0, i)
    return pl.pallas_call(
        _route_kernel,
        grid=(n // tb,),
        in_specs=[
            pl.BlockSpec((tb, d), lambda i: (i, 0)),
            pl.BlockSpec((d, dq), lambda i: (0, 0)),
            pl.BlockSpec((PEER_HEADS, N_KEYS, D_HALF), lambda i: (0, 0, 0)),
            pl.BlockSpec((PEER_HEADS, N_KEYS, D_HALF), lambda i: (0, 0, 0)),
        ],
        out_specs=[pl.BlockSpec((d, tb), tok)] + [pl.BlockSpec((2, rows, tb), lambda i: (0, 0, i))] * 2,
        out_shape=[jax.ShapeDtypeStruct((d, n), BF16)] + [jax.ShapeDtypeStruct((2, rows, n), F32)] * 2,
        scratch_shapes=[
            pltpu.VMEM((tb, dq), BF16),
            pltpu.VMEM((2, rows, tb), F32),
            pltpu.VMEM((2, TOPK * PEER_HEADS, tb), F32),
        ],
        compiler_params=_params("parallel"),
        name="peer_route",
    )(xn, wq, k1, k2)


def _peer_kernel(xt_ref, key2_ref, key1_ref, u_hbm, vt_ref, o_ref, h0, h1, a0, a1, u_buf, u_sem, *, nblk, last):
    eb, tb = h0.shape
    d = o_ref.shape[0]
    n_i1 = eb // N_KEYS
    rb = 2 * SUBLANES
    n_lb = tb // LANES
    hrows = eb // n_lb
    orows = d // n_lb
    s = pl.program_id(0)
    depth = u_buf.shape[0]

    def u_copy(t):
        rows = pl.ds(pl.multiple_of((jnp.minimum(t, last) % nblk) * eb, eb), eb)
        slot = t % depth
        return pltpu.make_async_copy(u_hbm.at[rows, :], u_buf.at[slot], u_sem.at[slot])

    @pl.when(s == 0)
    def _():
        for ref in (h0, h1, a0, a1):
            ref[...] = jnp.zeros_like(ref)
        for t in range(depth - 1):
            u_copy(t).start()

    @pl.when(s + depth - 1 <= last + 2)
    def _():
        u_copy(s + depth - 1).start()

    u_copy(s).wait()
    u_ref = u_buf.at[s % depth]

    @pl.when((s < 2) | ((s - 2) % nblk == 0))
    def _():
        o_ref[...] = jnp.zeros_like(o_ref)

    def step_part(h_new, h_prev, a_prev, a_old, lb):
        hr = pl.ds(lb * hrows, hrows)
        h_new[hr, :] = jnp.dot(u_ref[hr, :], xt_ref[...], preferred_element_type=F32)

        cols = pl.ds(lb * LANES, LANES)
        for r0 in range(0, N_KEYS, rb):
            gate = [None] * n_i1
            for h in range(PEER_HEADS):
                s2 = key2_ref[0, h, r0:r0 + rb, cols]
                e2 = key2_ref[1, h, r0:r0 + rb, cols]
                for a in range(n_i1):
                    t = jnp.where(s2 >= key1_ref[0, h, a:a + 1, cols], e2, 0.0) * key1_ref[1, h, a:a + 1, cols]
                    gate[a] = t if gate[a] is None else gate[a] + t
            for a in range(n_i1):
                hid = h_prev[a * N_KEYS + r0:a * N_KEYS + r0 + rb, cols]
                act = 0.5 * hid * (1.0 + lax.erf(hid * (1.0 / math.sqrt(2.0))))
                a_prev[a * N_KEYS + r0:a * N_KEYS + r0 + rb, cols] = (act * gate[a]).astype(BF16)

        orow = pl.ds(lb * orows, orows)
        o_ref[orow, :] += jnp.dot(vt_ref[orow, :], a_old[...], preferred_element_type=F32)

    @pl.when(s % 2 == 0)
    def _():
        for lb in range(n_lb):
            step_part(h0, h1, a1, a0, lb)

    @pl.when(s % 2 == 1)
    def _():
        for lb in range(n_lb):
            step_part(h1, h0, a0, a1, lb)


def _peer(xt, key2, key1, u, vt):
    d, n = xt.shape
    n_exp = u.shape[0]
    tb = _tile(n, ROWS_RESIDENT)
    eb = EXPERT_BLOCK
    n_i1 = eb // N_KEYS
    nblk = n_exp // eb
    n_tiles = n // tb
    last = n_tiles * nblk - 1
    r4 = lambda a: a.reshape(2, PEER_HEADS, N_KEYS, n)
    step = lambda s, lag: jnp.clip(s - lag, 0, last)
    tile = lambda s, lag: step(s, lag) // nblk
    blk = lambda s, lag: step(s, lag) % nblk
    return pl.pallas_call(
        functools.partial(_peer_kernel, nblk=nblk, last=last),
        grid=(n_tiles * nblk + 2,),
        in_specs=[
            pl.BlockSpec((d, tb), lambda s: (0, tile(s, 0)), pipeline_mode=pl.Buffered(1)),
            pl.BlockSpec((2, PEER_HEADS, N_KEYS, tb), lambda s: (0, 0, 0, tile(s, 1)), pipeline_mode=pl.Buffered(1)),
            pl.BlockSpec((2, PEER_HEADS, n_i1, tb), lambda s: (0, 0, blk(s, 1), tile(s, 1))),
            pl.BlockSpec(memory_space=pl.ANY),
            pl.BlockSpec((d, eb), lambda s: (0, blk(s, 2))),
        ],
        out_specs=pl.BlockSpec((d, tb), lambda s: (0, tile(s, 2))),
        out_shape=jax.ShapeDtypeStruct((d, n), F32),
        scratch_shapes=([pltpu.VMEM((eb, tb), F32)] * 2 + [pltpu.VMEM((eb, tb), BF16)] * 2
                        + [pltpu.VMEM((3, eb, d), F32), pltpu.SemaphoreType.DMA((3,))]),
        compiler_params=_params("arbitrary"),
        name="peer_dense",
    )(xt, r4(key2), r4(key1), u, vt)


def _final_kernel(h_ref, pt_ref, g_ref, y_ref, *, normalize):
    y = h_ref[...] + pt_ref[...].T
    if normalize:
        r = lax.rsqrt(jnp.mean(y * y, axis=-1, keepdims=True) + EPS)
        y = y * r * g_ref[...]
    y_ref[...] = y


def _final(h, pt, g, normalize):
    n, d = h.shape
    tm = _tile(n, ROWS_WIDE)
    return pl.pallas_call(
        functools.partial(_final_kernel, normalize=normalize),
        grid=(n // tm,),
        in_specs=[
            pl.BlockSpec((tm, d), lambda i: (i, 0)),
            pl.BlockSpec((d, tm), lambda i: (0, i)),
            pl.BlockSpec((1, d), lambda i: (0, 0)),
        ],
        out_specs=pl.BlockSpec((tm, d), lambda i: (i, 0)),
        out_shape=jax.ShapeDtypeStruct((n, d), F32),
        compiler_params=_params("parallel"),
        name="final",
    )(h, pt, g.reshape(1, d))


def _transpose_kernel(x_ref, o_ref):
    o_ref[...] = x_ref[...].T


def _transpose(x):
    r, c = x.shape
    tr = _tile(r, ROWS_WIDE)
    return pl.pallas_call(
        _transpose_kernel,
        grid=(r // tr,),
        in_specs=[pl.BlockSpec((tr, c), lambda i: (i, 0))],
        out_specs=pl.BlockSpec((c, tr), lambda i: (0, i)),
        out_shape=jax.ShapeDtypeStruct((c, r), x.dtype),
        compiler_params=_params("parallel"),
        name="transpose",
    )(x)


def _layer(x, seg_len, prev_k, prev_v, prev_conv, w, norm_final_g, last):
    n, d = x.shape
    n_seg = n // seg_len
    d_q = N_HEADS * HEAD_DIM
    d_kv = N_KV_HEADS * HEAD_DIM
    d_conv = w["conv_w"].shape[1]
    proj = _inproj(x, w["norm_mix_g"], w["w_in"])
    k_col, v_col = d_q // d_kv, d_q // d_kv + 1
    o_b = d_q + 2 * d_kv
    cb = COL_BLOCK
    b_col, c_col, x_col = o_b // cb, (o_b + d_conv) // cb, (o_b + 2 * d_conv) // cb
    ga_col, gb_col = (o_b + 3 * d_conv) // cb, (o_b + 3 * d_conv + d) // cb
    assert o_b % cb == 0 and d_conv % cb == 0 and d % cb == 0 and d_q % d_kv == 0

    if prev_k is None:
        tt = _tile(seg_len, ROWS_WIDE)
        assert tt % WINDOW == 0
        tps = seg_len // tt
        halo = lambda col: (lambda i: (jnp.maximum(i * (tt // WINDOW) - 1, 0), col))
        attn = _attention(w["attn_sinks"], proj, 0, proj, k_col, proj, v_col,
                          proj, halo(k_col), proj, halo(v_col), n, tt, tps)
        tc = _tile(seg_len, ROWS_STREAM)
        chalo = lambda col: (lambda i, j: (jnp.maximum(i * (tc // SUBLANES) - 1, 0), col + j))
        bc, utail = _conv(proj, b_col, c_col, x_col, proj, chalo(c_col), proj, chalo(x_col),
                          w["conv_w"], w["conv_b"], tc, seg_len // tc)
        new_k = proj.reshape(n_seg, seg_len, -1)[:, -WINDOW:, d_q:d_q + d_kv]
        new_v = proj.reshape(n_seg, seg_len, -1)[:, -WINDOW:, d_q + d_kv:d_q + 2 * d_kv]
    else:
        assert seg_len == CHUNK
        pk = prev_k.reshape(n_seg * WINDOW, d_kv)
        pv = prev_v.reshape(n_seg * WINDOW, d_kv)
        seg = lambda i: (i, 0)
        attn = _attention(w["attn_sinks"], proj, 0, proj, k_col, proj, v_col,
                          pk, seg, pv, seg, n, seg_len, 0)
        hist = jnp.pad(prev_conv, ((0, 0), (SUBLANES - (CONV_W - 1), 0), (0, 0))).reshape(n_seg * SUBLANES, d_conv)
        hmap = lambda i, j: (i, j)
        bc, utail = _conv(proj, b_col, c_col, x_col, hist, hmap, jnp.ones_like(hist), hmap,
                          w["conv_w"], w["conv_b"], seg_len, 0)
        k_new = proj[:, d_q:d_q + d_kv].reshape(n_seg, seg_len, d_kv)
        v_new = proj[:, d_q + d_kv:d_q + 2 * d_kv].reshape(n_seg, seg_len, d_kv)
        new_k = jnp.concatenate([prev_k.reshape(n_seg, WINDOW, d_kv), k_new], axis=1)[:, -WINDOW:]
        new_v = jnp.concatenate([prev_v.reshape(n_seg, WINDOW, d_kv), v_new], axis=1)[:, -WINDOW:]
    new_conv = utail.reshape(n_seg, -1, SUBLANES, d_conv)[:, -1, SUBLANES - (CONV_W - 1):]

    merged = _merge(attn, bc, w["w_proj_a"], w["w_proj_b"], proj, ga_col, gb_col)
    h, xn = _outproj(merged, x, w["w_out"], w["norm_ffn_g"])
    xt, key2, key1 = _route(xn, w["peer_w_query"], w["peer_keys1"], w["peer_keys2"])
    pt = _peer(xt, key2, key1, w["peer_u"], w["peer_vt"])
    y = _final(h, pt, norm_final_g, last)
    shape5 = (n_seg, WINDOW, N_KV_HEADS, HEAD_DIM)
    return y, new_k.reshape(shape5), new_v.reshape(shape5), new_conv


def kernel(x_prompt, x_sample, state_attn_k, state_attn_v, state_conv, norm_mix_g, w_in, attn_sinks, conv_w,
           conv_b, w_proj_a, w_proj_b, w_out, norm_ffn_g, peer_w_query, peer_keys1, peer_keys2, peer_u, peer_v,
           norm_final_g):
    depth = w_in.shape[0]
    bp, sp, d = x_prompt.shape
    bs, ss, _ = x_sample.shape
    yp = x_prompt.reshape(bp * sp, d)
    ys = x_sample.reshape(bs * ss, d)
    outs = [[] for _ in range(6)]
    for l in range(depth):
        w = dict(
            norm_mix_g=norm_mix_g[l], w_in=w_in[l].astype(BF16), attn_sinks=attn_sinks[l],
            conv_w=conv_w[l], conv_b=conv_b[l], w_proj_a=w_proj_a[l].astype(BF16),
            w_proj_b=w_proj_b[l].astype(BF16), w_out=w_out[l].astype(BF16), norm_ffn_g=norm_ffn_g[l],
            peer_w_query=peer_w_query[l].astype(BF16), peer_keys1=peer_keys1[l].astype(BF16),
            peer_keys2=peer_keys2[l].astype(BF16), peer_u=peer_u[l],
            peer_vt=_transpose(peer_v[l]),
        )
        last = l == depth - 1
        yp, k1, v1, c1 = _layer(yp, sp, None, None, None, w, norm_final_g, last)
        ys, k2, v2, c2 = _layer(ys, ss, state_attn_k[l], state_attn_v[l], state_conv[l], w, norm_final_g, last)
        for lst, val in zip(outs, (k1, v1, c1, k2, v2, c2)):
            lst.append(val)
    return (yp.reshape(bp, sp, d), ys.reshape(bs, ss, d)) + tuple(jnp.stack(o) for o in outs)
```
